```python
import jax
import jax.numpy as jnp
from jax import lax
import numpy as np

D_MODEL = 1024
BATCH = 16
SEQ = 2048
DEPTH = 4

CHUNK = 64
MEM_TOKENS = 256
N_EVEN = (DEPTH + 1) // 2
N_ODD = DEPTH // 2
EPS = 1e-6

A_HEADS = 4
A_HEAD_DIM = D_MODEL // 8
A_WIDTH = A_HEADS * A_HEAD_DIM
A_SUB = 8
A_NSUB = CHUNK // A_SUB

B_HEADS = 8
B_HEAD_DIM = D_MODEL // 16
B_WIDTH = B_HEADS * B_HEAD_DIM
B_PREV_CHUNKS = 8
B_BAND = B_PREV_CHUNKS + 1
REL_CLIP = 2 * CHUNK

AB_IN = 4 * A_WIDTH + 3 * B_WIDTH
AB_OUT = A_WIDTH + B_WIDTH
AB_SPLITS = (A_WIDTH, 2 * A_WIDTH, 3 * A_WIDTH, 4 * A_WIDTH,
             4 * A_WIDTH + B_WIDTH, 4 * A_WIDTH + 2 * B_WIDTH)

C_HEADS = 16
C_HEAD_DIM = D_MODEL // C_HEADS
C_WIDTH = C_HEADS * C_HEAD_DIM
C_IN = 3 * C_WIDTH + C_HEADS
Q_BLOCK = 128
FOX_BIAS_CENTER = 3.0

X_HEADS = 4
X_HEAD_DIM = D_MODEL // X_HEADS
X_WIDTH = X_HEADS * X_HEAD_DIM

D_FF = 4 * D_MODEL

kernel_name = 'hybrid_hgrn2_chunkattn_fox_encoder'


def rms_norm(x, gain):
    xf = x.astype(jnp.float32)
    y = xf * lax.rsqrt(jnp.mean(xf * xf, axis=-1, keepdims=True) + EPS)
    return (y * gain.astype(jnp.float32)).astype(x.dtype)


def split_heads(t, n_heads):
    return t.reshape(t.shape[0], t.shape[1], n_heads, -1)


def hgrn2_chunkwise(q, f_logit, i_val, lb):
    bsz, seq = q.shape[:2]
    n_chunks = seq // CHUNK
    z = f_logit.astype(jnp.float32)
    lb = lb.astype(jnp.float32)
    log_f = jnp.log(lb + (1.0 - lb) * jax.nn.sigmoid(z))
    k = (1.0 - lb) * jax.nn.sigmoid(-z)
    q = jax.nn.silu(q.astype(jnp.float32)) * A_HEAD_DIM ** -0.5
    v = i_val.astype(jnp.float32)

    def blocks(t):
        t = t.reshape(bsz, n_chunks, A_NSUB, A_SUB, A_HEADS, t.shape[-1])
        return t.transpose(0, 4, 1, 2, 3, 5)

    q, k, v, log_f = blocks(q), blocks(k), blocks(v), blocks(log_f)
    shp = log_f.shape
    b = jnp.cumsum(log_f.reshape(shp[:3] + (CHUNK, A_HEAD_DIM)), axis=3).reshape(shp)

    b_last = b[:, :, :, -1, -1]
    k_to_end = k * jnp.exp(b_last[:, :, :, None, None] - b)
    chunk_upd = jnp.einsum('bhnisk,bhnisv->bhnkv', k_to_end, v)

    def step(state, inp):
        decay, upd = inp
        return decay[..., None] * state + upd, state

    state0 = jnp.zeros((bsz, A_HEADS, A_HEAD_DIM, A_HEAD_DIM), jnp.float32)
    _, s_start = lax.scan(step, state0,
                          (jnp.moveaxis(jnp.exp(b_last), 2, 0), jnp.moveaxis(chunk_upd, 2, 0)))
    s_start = jnp.moveaxis(s_start, 0, 2)
    o = jnp.einsum('bhnitk,bhnkv->bhnitv', q * jnp.exp(b), s_start)

    r = b[:, :, :, :, -1]
    sub = jnp.arange(A_NSUB)
    earlier = (sub[:, None] > sub[None, :])[:, None, :, None]
    expo = b[:, :, :, :, :, None, :] - r[:, :, :, None, None, :, :]
    q_ref = q[:, :, :, :, :, None, :] * jnp.exp(jnp.where(earlier, expo, -jnp.inf))
    k_ref = k * jnp.exp(r[:, :, :, :, None, :] - b)
    a_off = jnp.einsum('bhnitjk,bhnjsk->bhnitjs', q_ref, k_ref)
    o = o + jnp.einsum('bhnitjs,bhnjsv->bhnitv', a_off, v)

    tri = jnp.tril(jnp.ones((A_SUB, A_SUB), dtype=bool))[:, :, None]
    pair = b[:, :, :, :, :, None, :] - b[:, :, :, :, None, :, :]
    decay = jnp.exp(jnp.where(tri, pair, -jnp.inf))
    a_diag = jnp.einsum('bhnitk,bhnitsk,bhnisk->bhnits', q, decay, k)
    o = o + jnp.einsum('bhnits,bhnisv->bhnitv', a_diag, v)
    return o.transpose(0, 2, 3, 4, 1, 5).reshape(bsz, seq, A_WIDTH)


def chunk_band_attention(q, k, v, rel_table):
    bsz, seq, n_heads, dh = q.shape
    n_chunks = seq // CHUNK
    qc = q.reshape(bsz, n_chunks, CHUNK, n_heads, dh)
    pad = ((0, 0), (B_PREV_CHUNKS, 0), (0, 0), (0, 0), (0, 0))
    kc = jnp.pad(k.reshape(bsz, n_chunks, CHUNK, n_heads, dh), pad)
    vc = jnp.pad(v.reshape(bsz, n_chunks, CHUNK, n_heads, dh), pad)
    band_idx = jnp.arange(n_chunks)[:, None] + jnp.arange(B_BAND)[None, :]
    kb = kc[:, band_idx].reshape(bsz, n_chunks, B_BAND * CHUNK, n_heads, dh)
    vb = vc[:, band_idx].reshape(bsz, n_chunks, B_BAND * CHUNK, n_heads, dh)
    logits = jnp.einsum('bnqhd,bnkhd->bhnqk', qc, kb).astype(jnp.float32) * dh ** -0.5
    offs = jnp.arange(B_BAND * CHUNK)
    rel = B_PREV_CHUNKS * CHUNK + jnp.arange(CHUNK)[:, None] - offs[None, :]
    rel_idx = jnp.clip(rel, -REL_CLIP, REL_CLIP) + REL_CLIP
    bias = rel_table.astype(jnp.float32)[:, rel_idx]
    valid = (jnp.arange(n_chunks)[:, None] - B_PREV_CHUNKS + offs[None, :] // CHUNK) >= 0
    logits = jnp.where(valid[None, None, :, None, :], logits + bias[None, :, None], -jnp.inf)
    p = jax.nn.softmax(logits, axis=-1).astype(v.dtype)
    out = jnp.einsum('bhnqk,bnkhd->bnqhd', p, vb)
    return out.reshape(bsz, seq, n_heads * dh)


def forgetting_attention(q, k, v, f_logit):
    bsz, seq, n_heads, dh = q.shape
    n_blocks = seq // Q_BLOCK
    cum = jnp.cumsum(jax.nn.log_sigmoid(f_logit.astype(jnp.float32)), axis=1)
    cum_k = cum.transpose(0, 2, 1)
    q_blocks = q.reshape(bsz, n_blocks, Q_BLOCK, n_heads, dh).transpose(1, 0, 2, 3, 4)
    c_blocks = cum.reshape(bsz, n_blocks, Q_BLOCK, n_heads).transpose(1, 0, 3, 2)
    pos_q = jnp.arange(seq).reshape(n_blocks, Q_BLOCK)
    pos_k = jnp.arange(seq)

    def one_block(args):
        q_blk, c_blk, p_blk = args
        s = jnp.einsum('bqhd,bkhd->bhqk', q_blk, k).astype(jnp.float32) * dh ** -0.5
        s = s + c_blk[..., None] - cum_k[:, :, None, :]
        s = jnp.where((p_blk[:, None] >= pos_k[None, :])[None, None], s, -jnp.inf)
        p = jax.nn.softmax(s, axis=-1).astype(v.dtype)
        return jnp.einsum('bhqk,bkhd->bqhd', p, v)

    out = lax.map(one_block, (q_blocks, c_blocks, pos_q))
    return out.transpose(1, 0, 2, 3, 4).reshape(bsz, seq, n_heads * dh)


def hgrn2_chunkattn_mixer(xn, w_in, lb, a_gain, rel_table, w_out):
    h = xn @ w_in
    q_a, f_a, i_a, g_a, q_b, k_b, v_b = jnp.split(h, AB_SPLITS, axis=-1)
    o_a = hgrn2_chunkwise(split_heads(q_a, A_HEADS), split_heads(f_a, A_HEADS),
                          split_heads(i_a, A_HEADS), lb.reshape(A_HEADS, A_HEAD_DIM))
    o_a = rms_norm(split_heads(o_a, A_HEADS), a_gain.reshape(A_HEADS, A_HEAD_DIM)).reshape(o_a.shape)
    o_a = (o_a * jax.nn.silu(g_a.astype(jnp.float32))).astype(xn.dtype)
    o_b = chunk_band_attention(split_heads(q_b, B_HEADS), split_heads(k_b, B_HEADS),
                               split_heads(v_b, B_HEADS), rel_table)
    return jnp.concatenate([o_a, o_b], axis=-1) @ w_out


def forgetting_mixer(xn, w_in, f_bias, w_out):
    h = xn @ w_in
    q, k, v, f = jnp.split(h, (C_WIDTH, 2 * C_WIDTH, 3 * C_WIDTH), axis=-1)
    o = forgetting_attention(split_heads(q, C_HEADS), split_heads(k, C_HEADS),
                             split_heads(v, C_HEADS), f + f_bias)
    return o @ w_out


def memory_cross_attention(xn, mem_n, w_q, w_kv, w_o):
    q = split_heads(xn @ w_q, X_HEADS)
    k, v = jnp.split(mem_n @ w_kv, 2, axis=-1)
    k, v = split_heads(k, X_HEADS), split_heads(v, X_HEADS)
    s = jnp.einsum('bqhd,bkhd->bhqk', q, k).astype(jnp.float32) * X_HEAD_DIM ** -0.5
    p = jax.nn.softmax(s, axis=-1).astype(v.dtype)
    o = jnp.einsum('bhqk,bkhd->bqhd', p, v)
    return o.reshape(xn.shape[0], xn.shape[1], X_WIDTH) @ w_o


def squared_relu_mlp(xn, w_up, w_down):
    return jnp.square(jax.nn.relu(xn @ w_up)) @ w_down


def setup_inputs(seed: int = 0) -> dict:
    key = jax.random.key(seed)
    ks = jax.random.split(key, 20)
    f32 = jnp.float32

    def dense(k, shape):
        return jax.random.normal(k, shape, f32) * shape[-2] ** -0.5

    def gain(k, shape):
        return 1.0 + 0.05 * jax.random.normal(k, shape, f32)

    return {
        'x': jax.random.normal(ks[0], (BATCH, SEQ, D_MODEL), f32),
        'mem': jax.random.normal(ks[1], (BATCH, MEM_TOKENS, D_MODEL), f32),
        'norm_mix': gain(ks[2], (DEPTH, D_MODEL)),
        'norm_xattn': gain(ks[3], (DEPTH, D_MODEL)),
        'norm_mem': gain(ks[4], (DEPTH, D_MODEL)),
        'norm_mlp': gain(ks[5], (DEPTH, D_MODEL)),
        'norm_final': gain(ks[6], (D_MODEL,)),
        'w_in_ab': dense(ks[7], (N_EVEN, D_MODEL, AB_IN)),
        'a_lb_logits': 0.5 * jax.random.normal(ks[8], (N_EVEN, A_WIDTH), f32),
        'a_out_gain': gain(ks[9], (N_EVEN, A_WIDTH)),
        'b_rel_bias': 0.2 * jax.random.normal(ks[10], (N_EVEN, B_HEADS, 2 * REL_CLIP + 1), f32),
        'w_out_ab': dense(ks[11], (N_EVEN, AB_OUT, D_MODEL)),
        'w_in_c': dense(ks[12], (N_ODD, D_MODEL, C_IN)),
        'c_fgate_bias': FOX_BIAS_CENTER + 0.5 * jax.random.normal(ks[13], (N_ODD, C_HEADS), f32),
        'w_out_c': dense(ks[14], (N_ODD, C_WIDTH, D_MODEL)),
        'w_xq': dense(ks[15], (DEPTH, D_MODEL, X_WIDTH)),
        'w_xkv': dense(ks[16], (DEPTH, D_MODEL, 2 * X_WIDTH)),
        'w_xo': dense(ks[17], (DEPTH, X_WIDTH, D_MODEL)),
        'w_up': dense(ks[18], (DEPTH, D_MODEL, D_FF)),
        'w_down': dense(ks[19], (DEPTH, D_FF, D_MODEL)),
    }


def reference(x, mem, norm_mix, norm_xattn, norm_mem, norm_mlp, norm_final,
              w_in_ab, a_lb_logits, a_out_gain, b_rel_bias, w_out_ab,
              w_in_c, c_fgate_bias, w_out_c, w_xq, w_xkv, w_xo, w_up, w_down):
    lb_w = jax.nn.softmax(a_lb_logits.astype(jnp.float32), axis=0)
    lower_bounds = jnp.cumsum(lb_w, axis=0) - lb_w[0]
    h = x
    for layer in range(DEPTH):
        xn = rms_norm(h, norm_mix[layer])
        if layer % 2 == 0:
            e = layer // 2
            mix = hgrn2_chunkattn_mixer(xn, w_in_ab[e], lower_bounds[e], a_out_gain[e],
                                        b_rel_bias[e], w_out_ab[e])
        else:
            o = layer // 2
            mix = forgetting_mixer(xn, w_in_c[o], c_fgate_bias[o], w_out_c[o])
        h = h + mix
        h = h + memory_cross_attention(rms_norm(h, norm_xattn[layer]), rms_norm(mem, norm_mem[layer]),
                                       w_xq[layer], w_xkv[layer], w_xo[layer])
        h = h + squared_relu_mlp(rms_norm(h, norm_mlp[layer]), w_up[layer], w_down[layer])
    return rms_norm(h, norm_final)
```

```python
import functools

import numpy as np
import jax
import jax.numpy as jnp
from jax import lax
from jax.experimental import pallas as pl
from jax.experimental.pallas import tpu as pltpu

F32 = jnp.float32
BF16 = jnp.bfloat16

EPS = 1e-6
CHUNK = 64
A_HEADS = 4
A_HEAD_DIM = 128
B_HEADS = 8
B_HEAD_DIM = 64
B_PREV_CHUNKS = 8
B_BAND = B_PREV_CHUNKS + 1
REL_CLIP = 2 * CHUNK
C_HEADS = 16
C_HEAD_DIM = 64
X_HEADS = 4
HGRN_LEVELS = 6

LANES = 128
VMEM_LIMIT_BYTES = 48 * 1024 * 1024

ROW_TILE = 512
FOX_BLOCK = 256

_NT = (((1,), (1,)), ((), ()))
_TN = (((0,), (0,)), ((), ()))


def _params(*semantics):
    return pltpu.CompilerParams(dimension_semantics=semantics, vmem_limit_bytes=VMEM_LIMIT_BYTES)


def _rms_norm(x, gain):
    return x * lax.rsqrt(jnp.mean(x * x, axis=-1, keepdims=True) + EPS) * gain


def _resident(shape):
    return pl.BlockSpec(shape, lambda *_: (0,) * len(shape))


def _norm_proj_kernel(x_ref, g_ref, w_ref, *out_refs):
    xn = _rms_norm(x_ref[...], g_ref[...]).astype(BF16)
    off = 0
    for o_ref in out_refs:
        width = o_ref.shape[-1]
        o_ref[...] = jnp.dot(xn, w_ref[:, off:off + width],
                             preferred_element_type=F32).astype(o_ref.dtype)
        off += width


def norm_proj(x, gain, w, splits, row_tile=ROW_TILE):
    rows, d = x.shape
    n = w.shape[1]
    assert sum(width for width, _ in splits) == n and rows % row_tile == 0
    return pl.pallas_call(
        _norm_proj_kernel,
        grid=(rows // row_tile,),
        in_specs=[pl.BlockSpec((row_tile, d), lambda i: (i, 0)),
                  _resident((1, d)), _resident((d, n))],
        out_specs=[pl.BlockSpec((row_tile, width), lambda i: (i, 0)) for width, _ in splits],
        out_shape=[jax.ShapeDtypeStruct((rows, width), dt) for width, dt in splits],
        compiler_params=_params("parallel"),
        name="norm_proj",
    )(x, gain.reshape(1, d), w)


def _proj_residual_kernel(n_in, h_ref, *refs):
    a_refs, w_refs, o_ref = refs[:n_in], refs[n_in:2 * n_in], refs[2 * n_in]
    acc = h_ref[...]
    for a_ref, w_ref in zip(a_refs, w_refs):
        acc = acc + jnp.dot(a_ref[...], w_ref[...], preferred_element_type=F32)
    o_ref[...] = acc


def proj_residual(h, acts, weights, row_tile=ROW_TILE):
    rows, d = h.shape
    in_specs = [pl.BlockSpec((row_tile, d), lambda i: (i, 0))]
    in_specs += [pl.BlockSpec((row_tile, a.shape[1]), lambda i: (i, 0)) for a in acts]
    in_specs += [_resident(w.shape) for w in weights]
    return pl.pallas_call(
        functools.partial(_proj_residual_kernel, len(acts)),
        grid=(rows // row_tile,),
        in_specs=in_specs,
        out_specs=pl.BlockSpec((row_tile, d), lambda i: (i, 0)),
        out_shape=jax.ShapeDtypeStruct((rows, d), F32),
        compiler_params=_params("parallel"),
        name="proj_residual",
    )(h, *acts, *weights)


def _mlp_kernel(ff_chunk, h_ref, g_ref, wu_ref, wd_ref, o_ref):
    h = h_ref[...]
    xn = _rms_norm(h, g_ref[...]).astype(BF16)
    acc = h
    for c in range(wu_ref.shape[1] // ff_chunk):
        cols = slice(c * ff_chunk, (c + 1) * ff_chunk)
        u = jnp.maximum(jnp.dot(xn, wu_ref[:, cols], preferred_element_type=F32), 0.0)
        acc = acc + jnp.dot((u * u).astype(BF16), wd_ref[cols, :], preferred_element_type=F32)
    o_ref[...] = acc


def mlp_sublayer(h, gain, w_up, w_down, row_tile=ROW_TILE, ff_chunk=1024):
    rows, d = h.shape
    return pl.pallas_call(
        functools.partial(_mlp_kernel, ff_chunk),
        grid=(rows // row_tile,),
        in_specs=[pl.BlockSpec((row_tile, d), lambda i: (i, 0)), _resident((1, d)),
                  _resident(w_up.shape), _resident(w_down.shape)],
        out_specs=pl.BlockSpec((row_tile, d), lambda i: (i, 0)),
        out_shape=jax.ShapeDtypeStruct((rows, d), F32),
        compiler_params=_params("parallel"),
        name="mlp",
    )(h, gain.reshape(1, d), w_up, w_down)


def _xattn_kernel(h_ref, g_ref, wq_ref, k_ref, v_ref, wo_ref, o_ref):
    h = h_ref[...]
    d = h.shape[-1]
    dh = d // X_HEADS
    xn = _rms_norm(h, g_ref[...]).astype(BF16)
    q = (jnp.dot(xn, wq_ref[...], preferred_element_type=F32) * dh ** -0.5).astype(BF16)
    outs = []
    for hd in range(X_HEADS):
        cols = slice(hd * dh, (hd + 1) * dh)
        s = lax.dot_general(q[:, cols], k_ref[0, :, cols], _NT, preferred_element_type=F32)
        p = jnp.exp(s - jnp.max(s, axis=-1, keepdims=True))
        denom = jnp.sum(p, axis=-1, keepdims=True)
        o = jnp.dot(p.astype(BF16), v_ref[0, :, cols], preferred_element_type=F32)
        outs.append((o / denom).astype(BF16))
    o_all = jnp.concatenate(outs, axis=-1)
    o_ref[...] = h + jnp.dot(o_all, wo_ref[...], preferred_element_type=F32)


def xattn_sublayer(h, gain, w_q, k_mem, v_mem, w_o, seq, row_tile=ROW_TILE):
    rows, d = h.shape
    tiles_per_seq = seq // row_tile
    mem_block = (1,) + k_mem.shape[1:]
    return pl.pallas_call(
        _xattn_kernel,
        grid=(rows // row_tile,),
        in_specs=[pl.BlockSpec((row_tile, d), lambda i: (i, 0)), _resident((1, d)),
                  _resident(w_q.shape),
                  pl.BlockSpec(mem_block, lambda i: (i // tiles_per_seq, 0, 0)),
                  pl.BlockSpec(mem_block, lambda i: (i // tiles_per_seq, 0, 0)),
                  _resident(w_o.shape)],
        out_specs=pl.BlockSpec((row_tile, d), lambda i: (i, 0)),
        out_shape=jax.ShapeDtypeStruct((rows, d), F32),
        compiler_params=_params("parallel"),
        name="xattn",
    )(h, gain.reshape(1, d), w_q, k_mem, v_mem, w_o)


def _hgrn_tables():
    c = CHUNK
    t = np.arange(c)[:, None]
    j = np.arange(c)[None, :]
    blocks = [(j <= t), (j > t)]
    masks = [np.eye(c, dtype=bool)]
    for level in range(HGRN_LEVELS):
        size = c >> level
        half = size // 2
        ref = (t // size) * size + half - 1
        upper = (t % size) >= half
        blocks.append(np.where(upper, (j > ref) & (j <= t), (j > t) & (j <= ref)))
        tt, ss = np.arange(c)[:, None], np.arange(c)[None, :]
        masks.append((tt // size == ss // size) & ((tt % size) >= half) & ((ss % size) < half))
    w = np.concatenate(blocks, axis=0).astype(np.float32)
    return np.concatenate([w, w], axis=1), np.stack(masks).astype(np.float32)


def _hgrn_kernel(q_ref, f_ref, i_ref, g_ref, lb_ref, gain_ref, ww_ref, mask_ref, o_ref, st_ref):
    c, dk = CHUNK, A_HEAD_DIM

    @pl.when(pl.program_id(1) == 0)
    def _():
        st_ref[...] = jnp.zeros_like(st_ref)

    row = lax.broadcasted_iota(jnp.int32, (c, dk), 0)
    for hd in range(A_HEADS):
        cols = slice(hd * dk, (hd + 1) * dk)
        z = f_ref[:, cols]
        lb = lb_ref[:, cols]
        log_f = jnp.log(lb + (1.0 - lb) * jax.nn.sigmoid(z))
        kk = (1.0 - lb) * jax.nn.sigmoid(-z)
        qf = q_ref[:, cols].astype(F32)
        qs = qf * jax.nn.sigmoid(qf) * dk ** -0.5
        v = i_ref[:, cols]

        hi = log_f.astype(BF16)
        lo = (log_f - hi.astype(F32)).astype(BF16)
        x = jnp.dot(ww_ref[...], jnp.concatenate([hi, lo], axis=0), preferred_element_type=F32)
        b = x[0:c]
        chunk_decay = jnp.exp(b[c - 1:c, :])
        q_in = (qs * jnp.exp(b)).astype(BF16)
        k_out = (kk * jnp.exp(x[c:2 * c])).astype(BF16)

        state_t = st_ref[hd]
        o = lax.dot_general(q_in, state_t.astype(BF16), _NT, preferred_element_type=F32)
        upd_t = lax.dot_general(v, k_out, _TN, preferred_element_type=F32)
        st_ref[hd] = state_t * chunk_decay + upd_t

        a = lax.dot_general(qs.astype(BF16), kk.astype(BF16), _NT,
                            preferred_element_type=F32) * mask_ref[0]
        for level in range(HGRN_LEVELS):
            half = (c >> level) // 2
            e = jnp.exp(x[(2 + level) * c:(3 + level) * c])
            m = (jnp.where((row & half) != 0, qs, kk) * e).astype(BF16)
            a = a + lax.dot_general(m, m, _NT, preferred_element_type=F32) * mask_ref[level + 1]
        o = o + jnp.dot(a.astype(BF16), v, preferred_element_type=F32)

        y = _rms_norm(o, gain_ref[:, cols])
        gf = g_ref[:, cols].astype(F32)
        o_ref[:, cols] = (y * (gf * jax.nn.sigmoid(gf))).astype(o_ref.dtype)


def hgrn2_mixer(q, f, i, g, lower_bound, out_gain, batch, seq):
    rows, width = q.shape
    n_chunks = seq // CHUNK
    ww, masks = _hgrn_tables()
    act = pl.BlockSpec((CHUNK, width), lambda b, n: (b * n_chunks + n, 0))
    return pl.pallas_call(
        _hgrn_kernel,
        grid=(batch, n_chunks),
        in_specs=[act, act, act, act, _resident((1, width)), _resident((1, width)),
                  _resident(ww.shape), _resident(masks.shape)],
        out_specs=act,
        out_shape=jax.ShapeDtypeStruct((rows, width), BF16),
        scratch_shapes=[pltpu.VMEM((A_HEADS, A_HEAD_DIM, A_HEAD_DIM), F32)],
        compiler_params=_params("parallel", "arbitrary"),
        name="hgrn2",
    )(q, f, i, g, lower_bound.reshape(1, width), out_gain.reshape(1, width),
      jnp.asarray(ww, BF16), jnp.asarray(masks))


def _rel_bias_kernel(table_ref, o_ref):
    t = pl.program_id(0)
    n_idx, band = table_ref.shape[1], o_ref.shape[-1]
    off = lax.broadcasted_iota(jnp.int32, (n_idx, band), 1)
    rel = B_PREV_CHUNKS * CHUNK + t - off
    idx = jnp.clip(rel, -REL_CLIP, REL_CLIP) + REL_CLIP
    onehot = (lax.broadcasted_iota(jnp.int32, (n_idx, band), 0) == idx).astype(F32)
    o_ref[0] = jnp.dot(table_ref[...], onehot, preferred_element_type=F32,
                       precision=lax.Precision.HIGHEST)


def rel_bias(rel_table):
    heads, n_rel = rel_table.shape
    n_idx = -(-n_rel // LANES) * LANES
    table = jnp.pad(rel_table.astype(F32), ((0, 0), (0, n_idx - n_rel)))
    band = B_BAND * CHUNK
    out = pl.pallas_call(
        _rel_bias_kernel,
        grid=(CHUNK,),
        in_specs=[_resident((heads, n_idx))],
        out_specs=pl.BlockSpec((1, heads, band), lambda t: (t, 0, 0)),
        out_shape=jax.ShapeDtypeStruct((CHUNK, heads, band), F32),
        compiler_params=_params("parallel"),
        name="rel_bias",
    )(table)
    return out.transpose(1, 0, 2)


def _band_attn_kernel(q_ref, k_ref, v_ref, bias_ref, o_ref):
    n = pl.program_id(1)
    c, dh = CHUNK, B_HEAD_DIM
    band = B_BAND * c
    starts = [pl.multiple_of(jnp.maximum(n - B_PREV_CHUNKS + j, 0) * c, c) for j in range(B_BAND)]
    k_band = jnp.concatenate([k_ref[0, pl.ds(s, c), :] for s in starts], axis=0)
    v_band = jnp.concatenate([v_ref[0, pl.ds(s, c), :] for s in starts], axis=0)
    valid = lax.broadcasted_iota(jnp.int32, (c, band), 1) >= (B_PREV_CHUNKS - n) * c
    lane = lax.broadcasted_iota(jnp.int32, (c, 2 * dh), 1)
    q_all = q_ref[...] * jnp.asarray(dh ** -0.5, q_ref.dtype)
    for pair in range(B_HEADS // 2):
        cols = slice(pair * 2 * dh, (pair + 1) * 2 * dh)
        q_pair, k_pair, v_pair = q_all[:, cols], k_band[:, cols], v_band[:, cols]
        outs = []
        for sub in range(2):
            own = (lane < dh) if sub == 0 else (lane >= dh)
            q_h = jnp.where(own, q_pair, jnp.zeros_like(q_pair))
            s = lax.dot_general(q_h, k_pair, _NT, preferred_element_type=F32)
            s = jnp.where(valid, s + bias_ref[2 * pair + sub], -jnp.inf)
            p = jnp.exp(s - jnp.max(s, axis=-1, keepdims=True))
            denom = jnp.sum(p, axis=-1, keepdims=True)
            outs.append(jnp.dot(p.astype(BF16), v_pair, preferred_element_type=F32) / denom)
        o_ref[:, cols] = jnp.where(lane < dh, outs[0], outs[1]).astype(o_ref.dtype)


def band_attention(q, k, v, bias, batch, seq):
    rows, width = q.shape
    n_chunks = seq // CHUNK
    kv_spec = pl.BlockSpec((1, seq, width), lambda b, n: (b, 0, 0))
    return pl.pallas_call(
        _band_attn_kernel,
        grid=(batch, n_chunks),
        in_specs=[pl.BlockSpec((CHUNK, width), lambda b, n: (b * n_chunks + n, 0)),
                  kv_spec, kv_spec, _resident(bias.shape)],
        out_specs=pl.BlockSpec((CHUNK, width), lambda b, n: (b * n_chunks + n, 0)),
        out_shape=jax.ShapeDtypeStruct((rows, width), BF16),
        compiler_params=_params("parallel", "arbitrary"),
        name="band_attn",
    )(q, k.reshape(batch, seq, width), v.reshape(batch, seq, width), bias)


def _fox_cum_kernel(blk, f_ref, bias_ref, tril_ref, o_ref):
    seq = f_ref.shape[1]
    carry = jnp.zeros((1, f_ref.shape[2]), F32)
    for c in range(seq // blk):
        x = f_ref[0, c * blk:(c + 1) * blk, :] + bias_ref[...]
        log_sig = jnp.minimum(x, 0.0) - jnp.log1p(jnp.exp(-jnp.abs(x)))
        cs = jnp.dot(tril_ref[...], log_sig, preferred_element_type=F32,
                     precision=lax.Precision.HIGHEST) + carry
        carry = cs[blk - 1:blk, :]
        o_ref[0, :, c * blk:(c + 1) * blk] = cs.T


def fox_cum(f, f_bias, batch, seq, blk=256):
    width = f.shape[-1]
    bias = jnp.pad(f_bias.astype(F32), (0, width - f_bias.shape[0])).reshape(1, width)
    tril = jnp.asarray(np.tril(np.ones((blk, blk), np.float32)))
    return pl.pallas_call(
        functools.partial(_fox_cum_kernel, blk),
        grid=(batch,),
        in_specs=[pl.BlockSpec((1, seq, width), lambda b: (b, 0, 0)), _resident((1, width)),
                  _resident((blk, blk))],
        out_specs=pl.BlockSpec((1, width, seq), lambda b: (b, 0, 0)),
        out_shape=jax.ShapeDtypeStruct((batch, width, seq), F32),
        compiler_params=_params("parallel"),
        name="fox_cum",
    )(f.reshape(batch, seq, width), bias, tril)


def _fox_attn_kernel(q_ref, k_ref, v_ref, cum_ref, o_ref, m_ref, l_ref, acc_ref):
    qi = pl.program_id(2)
    blk, dh = q_ref.shape[1], C_HEAD_DIM
    lane = lax.broadcasted_iota(jnp.int32, (blk, 2 * dh), 1)
    q_pair = q_ref[0] * jnp.asarray(dh ** -0.5, q_ref.dtype)
    q_heads = [jnp.where(lane < dh, q_pair, jnp.zeros_like(q_pair)),
               jnp.where(lane >= dh, q_pair, jnp.zeros_like(q_pair))]
    m_ref[...] = jnp.full_like(m_ref, -jnp.inf)
    l_ref[...] = jnp.zeros_like(l_ref)
    acc_ref[...] = jnp.zeros_like(acc_ref)
    causal = (lax.broadcasted_iota(jnp.int32, (blk, blk), 0)
              >= lax.broadcasted_iota(jnp.int32, (blk, blk), 1))

    def step(j, diagonal):
        rows = pl.ds(pl.multiple_of(j * blk, blk), blk)
        k_j, v_j = k_ref[0, rows, :], v_ref[0, rows, :]
        cum = cum_ref[0, 0, j]
        for sub in range(2):
            s = lax.dot_general(q_heads[sub], k_j, _NT, preferred_element_type=F32)
            s = s - cum[sub:sub + 1, :]
            if diagonal:
                s = jnp.where(causal, s, -jnp.inf)
            m_prev = m_ref[sub]
            m_new = jnp.maximum(m_prev, jnp.max(s, axis=-1, keepdims=True))
            p = jnp.exp(s - jnp.concatenate([m_new] * (blk // LANES), axis=-1))
            alpha = jnp.exp(m_prev - m_new)
            l_ref[sub] = alpha * l_ref[sub] + jnp.sum(p, axis=-1, keepdims=True)
            acc_ref[sub] = alpha * acc_ref[sub] + jnp.dot(p.astype(BF16), v_j,
                                                          preferred_element_type=F32)
            m_ref[sub] = m_new

    def body(j, carry):
        step(j, False)
        return carry

    lax.fori_loop(0, qi, body, 0)
    step(qi, True)
    o_ref[0] = jnp.where(lane < dh, acc_ref[0] / l_ref[0], acc_ref[1] / l_ref[1]).astype(o_ref.dtype)


def fox_attention(q, k, v, cum_t, batch, seq, blk=FOX_BLOCK):
    width = q.shape[-1]
    pairs = C_HEADS // 2
    n_blk = seq // blk
    pair_w = 2 * C_HEAD_DIM
    cum = cum_t[:, :C_HEADS, :].reshape(batch, pairs, 2, n_blk, blk).transpose(0, 1, 3, 2, 4)
    q3, k3, v3 = (t.reshape(batch, seq, width) for t in (q, k, v))
    kv_spec = pl.BlockSpec((1, seq, pair_w), lambda b, p, i: (b, 0, p))
    out = pl.pallas_call(
        _fox_attn_kernel,
        grid=(batch, pairs, n_blk),
        in_specs=[pl.BlockSpec((1, blk, pair_w), lambda b, p, i: (b, i, p)), kv_spec, kv_spec,
                  pl.BlockSpec((1, 1, n_blk, 2, blk), lambda b, p, i: (b, p, 0, 0, 0))],
        out_specs=pl.BlockSpec((1, blk, pair_w), lambda b, p, i: (b, i, p)),
        out_shape=jax.ShapeDtypeStruct((batch, seq, width), BF16),
        scratch_shapes=[pltpu.VMEM((2, blk, pair_w), F32)] * 3,
        compiler_params=_params("parallel", "parallel", "arbitrary"),
        name="fox_attn",
    )(q3, k3, v3, cum)
    return out.reshape(batch * seq, width)


def _final_norm_kernel(x_ref, g_ref, o_ref):
    o_ref[...] = _rms_norm(x_ref[...], g_ref[...])


def final_norm(x, gain, row_tile=ROW_TILE):
    rows, d = x.shape
    return pl.pallas_call(
        _final_norm_kernel,
        grid=(rows // row_tile,),
        in_specs=[pl.BlockSpec((row_tile, d), lambda i: (i, 0)), _resident((1, d))],
        out_specs=pl.BlockSpec((row_tile, d), lambda i: (i, 0)),
        out_shape=jax.ShapeDtypeStruct((rows, d), F32),
        compiler_params=_params("parallel"),
        name="final_norm",
    )(x, gain.reshape(1, d))


def _even_mixer(h, gain, w_in, lower_bound, out_gain, rel_table, w_out, batch, seq):
    a_w, b_w = A_HEADS * A_HEAD_DIM, B_HEADS * B_HEAD_DIM
    splits = [(a_w, BF16), (a_w, F32), (a_w, BF16), (a_w, BF16),
              (b_w, BF16), (b_w, BF16), (b_w, BF16)]
    q_a, f_a, i_a, g_a, q_b, k_b, v_b = norm_proj(h, gain, w_in.astype(BF16), splits)
    o_a = hgrn2_mixer(q_a, f_a, i_a, g_a, lower_bound, out_gain, batch, seq)
    o_b = band_attention(q_b, k_b, v_b, rel_bias(rel_table), batch, seq)
    w_out = w_out.astype(BF16)
    return proj_residual(h, [o_a, o_b], [w_out[:a_w], w_out[a_w:]])


def _odd_mixer(h, gain, w_in, f_bias, w_out, batch, seq):
    c_w = C_HEADS * C_HEAD_DIM
    w_pad = jnp.pad(w_in, ((0, 0), (0, LANES - C_HEADS))).astype(BF16)
    splits = [(c_w, BF16), (c_w, BF16), (c_w, BF16), (LANES, F32)]
    q, k, v, f = norm_proj(h, gain, w_pad, splits)
    cum_t = fox_cum(f, f_bias, batch, seq)
    o = fox_attention(q, k, v, cum_t, batch, seq)
    return proj_residual(h, [o], [w_out.astype(BF16)])


def kernel(x, mem, norm_mix, norm_xattn, norm_mem, norm_mlp, norm_final, w_in_ab, a_lb_logits,
           a_out_gain, b_rel_bias, w_out_ab, w_in_c, c_fgate_bias, w_out_c, w_xq, w_xkv, w_xo,
           w_up, w_down):
    batch, seq, d = x.shape
    mem_tokens = mem.shape[1]
    depth = norm_mix.shape[0]
    lb_w = jax.nn.softmax(a_lb_logits.astype(F32), axis=0)
    lower_bounds = jnp.cumsum(lb_w, axis=0) - lb_w[0]
    h = x.reshape(batch * seq, d)
    mem2 = mem.reshape(batch * mem_tokens, d)
    for layer in range(depth):
        if layer % 2 == 0:
            e = layer // 2
            h = _even_mixer(h, norm_mix[layer], w_in_ab[e], lower_bounds[e], a_out_gain[e],
                            b_rel_bias[e], w_out_ab[e], batch, seq)
        else:
            o = layer // 2
            h = _odd_mixer(h, norm_mix[layer], w_in_c[o], c_fgate_bias[o], w_out_c[o], batch, seq)
        k_mem, v_mem = norm_proj(mem2, norm_mem[layer], w_xkv[layer].astype(BF16),
                                 [(d, BF16), (d, BF16)])
        h = xattn_sublayer(h, norm_xattn[layer], w_xq[layer].astype(BF16),
                           k_mem.reshape(batch, mem_tokens, d), v_mem.reshape(batch, mem_tokens, d),
                           w_xo[layer].astype(BF16), seq)
        h = mlp_sublayer(h, norm_mlp[layer], w_up[layer].astype(BF16), w_down[layer].astype(BF16))
    return final_norm(h, norm_final).reshape(batch, seq, d)
```

```python
import functools

import numpy as np
import jax
import jax.numpy as jnp
from jax import lax
from jax.experimental import pallas as pl
from jax.experimental.pallas import tpu as pltpu

F32 = jnp.float32
BF16 = jnp.bfloat16

EPS = 1e-6
CHUNK = 64
A_HEADS = 4
A_HEAD_DIM = 128
B_HEADS = 8
B_HEAD_DIM = 64
B_PREV_CHUNKS = 8
B_BAND = B_PREV_CHUNKS + 1
REL_CLIP = 2 * CHUNK
C_HEADS = 16
C_HEAD_DIM = 64
X_HEADS = 4
HGRN_LEVELS = 6

LANES = 128
VMEM_LIMIT_BYTES = 48 * 1024 * 1024

ROW_TILE = 512
FOX_BLOCK = 256

_NT = (((1,), (1,)), ((), ()))
_TN = (((0,), (0,)), ((), ()))


def _params(*semantics):
    return pltpu.CompilerParams(dimension_semantics=semantics, vmem_limit_bytes=VMEM_LIMIT_BYTES)


def _rms_norm(x, gain):
    return x * lax.rsqrt(jnp.mean(x * x, axis=-1, keepdims=True) + EPS) * gain


def _resident(shape):
    return pl.BlockSpec(shape, lambda *_: (0,) * len(shape))


def _norm_proj_kernel(x_ref, g_ref, w_ref, *out_refs):
    xn = _rms_norm(x_ref[...], g_ref[...]).astype(BF16)
    off = 0
    for o_ref in out_refs:
        width = o_ref.shape[-1]
        o_ref[...] = jnp.dot(xn, w_ref[:, off:off + width],
                             preferred_element_type=F32).astype(o_ref.dtype)
        off += width


def norm_proj(x, gain, w, splits, row_tile=ROW_TILE):
    rows, d = x.shape
    n = w.shape[1]
    assert sum(width for width, _ in splits) == n and rows % row_tile == 0
    return pl.pallas_call(
        _norm_proj_kernel,
        grid=(rows // row_tile,),
        in_specs=[pl.BlockSpec((row_tile, d), lambda i: (i, 0)),
                  _resident((1, d)), _resident((d, n))],
        out_specs=[pl.BlockSpec((row_tile, width), lambda i: (i, 0)) for width, _ in splits],
        out_shape=[jax.ShapeDtypeStruct((rows, width), dt) for width, dt in splits],
        compiler_params=_params("parallel"),
        name="norm_proj",
    )(x, gain.reshape(1, d), w)


def _proj_residual_kernel(n_in, h_ref, *refs):
    a_refs, w_refs, o_ref = refs[:n_in], refs[n_in:2 * n_in], refs[2 * n_in]
    acc = h_ref[...]
    for a_ref, w_ref in zip(a_refs, w_refs):
        acc = acc + jnp.dot(a_ref[...], w_ref[...], preferred_element_type=F32)
    o_ref[...] = acc


def proj_residual(h, acts, weights, row_tile=ROW_TILE):
    rows, d = h.shape
    in_specs = [pl.BlockSpec((row_tile, d), lambda i: (i, 0))]
    in_specs += [pl.BlockSpec((row_tile, a.shape[1]), lambda i: (i, 0)) for a in acts]
    in_specs += [_resident(w.shape) for w in weights]
    return pl.pallas_call(
        functools.partial(_proj_residual_kernel, len(acts)),
        grid=(rows // row_tile,),
        in_specs=in_specs,
        out_specs=pl.BlockSpec((row_tile, d), lambda i: (i, 0)),
        out_shape=jax.ShapeDtypeStruct((rows, d), F32),
        compiler_params=_params("parallel"),
        name="proj_residual",
    )(h, *acts, *weights)


def _mlp_kernel(ff_chunk, h_ref, g_ref, wu_ref, wd_ref, o_ref):
    h = h_ref[...]
    xn = _rms_norm(h, g_ref[...]).astype(BF16)
    acc = h
    for c in range(wu_ref.shape[1] // ff_chunk):
        cols = slice(c * ff_chunk, (c + 1) * ff_chunk)
        u = jnp.maximum(jnp.dot(xn, wu_ref[:, cols], preferred_element_type=F32), 0.0)
        acc = acc + jnp.dot((u * u).astype(BF16), wd_ref[cols, :], preferred_element_type=F32)
    o_ref[...] = acc


def mlp_sublayer(h, gain, w_up, w_down, row_tile=ROW_TILE, ff_chunk=1024):
    rows, d = h.shape
    return pl.pallas_call(
        functools.partial(_mlp_kernel, ff_chunk),
        grid=(rows // row_tile,),
        in_specs=[pl.BlockSpec((row_tile, d), lambda i: (i, 0)), _resident((1, d)),
                  _resident(w_up.shape), _resident(w_down.shape)],
        out_specs=pl.BlockSpec((row_tile, d), lambda i: (i, 0)),
        out_shape=jax.ShapeDtypeStruct((rows, d), F32),
        compiler_params=_params("parallel"),
        name="mlp",
    )(h, gain.reshape(1, d), w_up, w_down)


def _xattn_kernel(h_ref, g_ref, wq_ref, k_ref, v_ref, wo_ref, o_ref):
    h = h_ref[...]
    d = h.shape[-1]
    dh = d // X_HEADS
    xn = _rms_norm(h, g_ref[...]).astype(BF16)
    q = (jnp.dot(xn, wq_ref[...], preferred_element_type=F32) * dh ** -0.5).astype(BF16)
    outs = []
    for hd in range(X_HEADS):
        cols = slice(hd * dh, (hd + 1) * dh)
        s = lax.dot_general(q[:, cols], k_ref[0, :, cols], _NT, preferred_element_type=F32)
        p = jnp.exp(s - jnp.max(s, axis=-1, keepdims=True))
        denom = jnp.sum(p, axis=-1, keepdims=True)
        o = jnp.dot(p.astype(BF16), v_ref[0, :, cols], preferred_element_type=F32)
        outs.append((o / denom).astype(BF16))
    o_all = jnp.concatenate(outs, axis=-1)
    o_ref[...] = h + jnp.dot(o_all, wo_ref[...], preferred_element_type=F32)


def xattn_sublayer(h, gain, w_q, k_mem, v_mem, w_o, seq, row_tile=ROW_TILE):
    rows, d = h.shape
    tiles_per_seq = seq // row_tile
    mem_block = (1,) + k_mem.shape[1:]
    return pl.pallas_call(
        _xattn_kernel,
        grid=(rows // row_tile,),
        in_specs=[pl.BlockSpec((row_tile, d), lambda i: (i, 0)), _resident((1, d)),
                  _resident(w_q.shape),
                  pl.BlockSpec(mem_block, lambda i: (i // tiles_per_seq, 0, 0)),
                  pl.BlockSpec(mem_block, lambda i: (i // tiles_per_seq, 0, 0)),
                  _resident(w_o.shape)],
        out_specs=pl.BlockSpec((row_tile, d), lambda i: (i, 0)),
        out_shape=jax.ShapeDtypeStruct((rows, d), F32),
        compiler_params=_params("parallel"),
        name="xattn",
    )(h, gain.reshape(1, d), w_q, k_mem, v_mem, w_o)


def _hgrn_tables():
    c = CHUNK
    t = np.arange(c)[:, None]
    j = np.arange(c)[None, :]
    blocks = [(j <= t), (j > t)]
    masks = [np.eye(c, dtype=bool)]
    for level in range(HGRN_LEVELS):
        size = c >> level
        half = size // 2
        ref = (t // size) * size + half - 1
        upper = (t % size) >= half
        blocks.append(np.where(upper, (j > ref) & (j <= t), (j > t) & (j <= ref)))
        tt, ss = np.arange(c)[:, None], np.arange(c)[None, :]
        masks.append((tt // size == ss // size) & ((tt % size) >= half) & ((ss % size) < half))
    w = np.concatenate(blocks, axis=0).astype(np.float32)
    return np.concatenate([w, w], axis=1), np.stack(masks).astype(np.float32)


def _hgrn_kernel(q_ref, f_ref, i_ref, g_ref, lb_ref, gain_ref, ww_ref, mask_ref, o_ref, st_ref):
    c, dk = CHUNK, A_HEAD_DIM

    @pl.when(pl.program_id(1) == 0)
    def _():
        st_ref[...] = jnp.zeros_like(st_ref)

    row = lax.broadcasted_iota(jnp.int32, (c, dk), 0)
    for hd in range(A_HEADS):
        cols = slice(hd * dk, (hd + 1) * dk)
        z = f_ref[:, cols]
        lb = lb_ref[:, cols]
        log_f = jnp.log(lb + (1.0 - lb) * jax.nn.sigmoid(z))
        kk = (1.0 - lb) * jax.nn.sigmoid(-z)
        qf = q_ref[:, cols].astype(F32)
        qs = qf * jax.nn.sigmoid(qf) * dk ** -0.5
        v = i_ref[:, cols]

        hi = log_f.astype(BF16)
        lo = (log_f - hi.astype(F32)).astype(BF16)
        x = jnp.dot(ww_ref[...], jnp.concatenate([hi, lo], axis=0), preferred_element_type=F32)
        b = x[0:c]
        chunk_decay = jnp.exp(b[c - 1:c, :])
        q_in = (qs * jnp.exp(b)).astype(BF16)
        k_out = (kk * jnp.exp(x[c:2 * c])).astype(BF16)

        state_t = st_ref[hd]
        o = lax.dot_general(q_in, state_t.astype(BF16), _NT, preferred_element_type=F32)
        upd_t = lax.dot_general(v, k_out, _TN, preferred_element_type=F32)
        st_ref[hd] = state_t * chunk_decay + upd_t

        a = lax.dot_general(qs.astype(BF16), kk.astype(BF16), _NT,
                            preferred_element_type=F32) * mask_ref[0]
        for level in range(HGRN_LEVELS):
            half = (c >> level) // 2
            e = jnp.exp(x[(2 + level) * c:(3 + level) * c])
            m = (jnp.where((row & half) != 0, qs, kk) * e).astype(BF16)
            a = a + lax.dot_general(m, m, _NT, preferred_element_type=F32) * mask_ref[level + 1]
        o = o + jnp.dot(a.astype(BF16), v, preferred_element_type=F32)

        y = _rms_norm(o, gain_ref[:, cols])
        gf = g_ref[:, cols].astype(F32)
        o_ref[:, cols] = (y * (gf * jax.nn.sigmoid(gf))).astype(o_ref.dtype)


def hgrn2_mixer(q, f, i, g, lower_bound, out_gain, batch, seq):
    rows, width = q.shape
    n_chunks = seq // CHUNK
    ww, masks = _hgrn_tables()
    act = pl.BlockSpec((CHUNK, width), lambda b, n: (b * n_chunks + n, 0))
    return pl.pallas_call(
        _hgrn_kernel,
        grid=(batch, n_chunks),
        in_specs=[act, act, act, act, _resident((1, width)), _resident((1, width)),
                  _resident(ww.shape), _resident(masks.shape)],
        out_specs=act,
        out_shape=jax.ShapeDtypeStruct((rows, width), BF16),
        scratch_shapes=[pltpu.VMEM((A_HEADS, A_HEAD_DIM, A_HEAD_DIM), F32)],
        compiler_params=_params("parallel", "arbitrary"),
        name="hgrn2",
    )(q, f, i, g, lower_bound.reshape(1, width), out_gain.reshape(1, width),
      jnp.asarray(ww, BF16), jnp.asarray(masks))


BAND_GROUP = 4
BAND_ROWS = BAND_GROUP * CHUNK
BAND_WINDOW = (BAND_GROUP + B_PREV_CHUNKS) * CHUNK
BIAS_SPAN = BAND_ROWS + BAND_WINDOW


def _rel_bias_kernel(table_ref, o_ref):
    heads, n_idx = table_ref.shape
    e = lax.broadcasted_iota(jnp.int32, (n_idx, BIAS_SPAN), 1)
    distance = B_PREV_CHUNKS * CHUNK + BAND_ROWS - 1 - e
    idx = jnp.clip(distance, -REL_CLIP, REL_CLIP) + REL_CLIP
    onehot = (lax.broadcasted_iota(jnp.int32, (n_idx, BIAS_SPAN), 0) == idx).astype(F32)
    by_offset = jnp.dot(table_ref[...], onehot, preferred_element_type=F32,
                        precision=lax.Precision.HIGHEST)
    col = lax.broadcasted_iota(jnp.int32, (heads, BAND_WINDOW), 1)
    for r in range(BAND_ROWS):
        band_start = (r // CHUNK) * CHUNK
        in_band = (col >= band_start) & (col < band_start + B_BAND * CHUNK)
        shift = BAND_ROWS - 1 - r
        o_ref[:, r * BAND_WINDOW:(r + 1) * BAND_WINDOW] = jnp.where(
            in_band, by_offset[:, shift:shift + BAND_WINDOW], -jnp.inf)


def rel_bias(rel_table):
    heads, n_rel = rel_table.shape
    n_idx = -(-n_rel // LANES) * LANES
    table = jnp.pad(rel_table.astype(F32), ((0, 0), (0, n_idx - n_rel)))
    out = pl.pallas_call(
        _rel_bias_kernel,
        in_specs=[pl.BlockSpec(memory_space=pltpu.VMEM)],
        out_specs=pl.BlockSpec(memory_space=pltpu.VMEM),
        out_shape=jax.ShapeDtypeStruct((heads, BAND_ROWS * BAND_WINDOW), F32),
        compiler_params=pltpu.CompilerParams(vmem_limit_bytes=VMEM_LIMIT_BYTES),
        name="rel_bias",
    )(table)
    return out.reshape(heads, BAND_ROWS, BAND_WINDOW)


def _band_attn_kernel(q_ref, k_ref, v_ref, bias_ref, o_ref):
    g = pl.program_id(1)
    c, dh = CHUNK, B_HEAD_DIM
    n_slots = BAND_WINDOW // c
    first = g * BAND_GROUP - B_PREV_CHUNKS
    lane = lax.broadcasted_iota(jnp.int32, (BAND_ROWS, 2 * dh), 1)
    lane_kv = lax.broadcasted_iota(jnp.int32, (BAND_WINDOW, 2 * dh), 1)

    def attend(at_sequence_start):
        starts = [pl.multiple_of(jnp.maximum(first + j, 0) * c, c) for j in range(n_slots)]
        k_win = jnp.concatenate([k_ref[0, pl.ds(s, c), :] for s in starts], axis=0)
        v_win = jnp.concatenate([v_ref[0, pl.ds(s, c), :] for s in starts], axis=0)
        started = lax.broadcasted_iota(jnp.int32, (BAND_ROWS, BAND_WINDOW), 1) >= -first * c
        q_all = q_ref[...] * jnp.asarray(dh ** -0.5, q_ref.dtype)
        for pair in range(B_HEADS // 2):
            cols = slice(pair * 2 * dh, (pair + 1) * 2 * dh)
            q_pair, k_pair, v_pair = q_all[:, cols], k_win[:, cols], v_win[:, cols]
            outs = []
            for sub in range(2):
                own = (lane < dh) if sub == 0 else (lane >= dh)
                q_h = jnp.where(own, q_pair, jnp.zeros_like(q_pair))
                s = lax.dot_general(q_h, k_pair, _NT, preferred_element_type=F32)
                s = s + bias_ref[2 * pair + sub]
                if at_sequence_start:
                    s = jnp.where(started, s, -jnp.inf)
                p = jnp.exp(s - jnp.max(s, axis=-1, keepdims=True)).astype(BF16)
                ones_lane = dh if sub == 0 else 0
                own_kv = (lane_kv < dh) if sub == 0 else (lane_kv >= dh)
                v_h = jnp.where(own_kv, v_pair,
                                jnp.where(lane_kv == ones_lane, 1.0, 0.0).astype(v_pair.dtype))
                o = jnp.dot(p, v_h, preferred_element_type=F32)
                outs.append(o / o[:, ones_lane:ones_lane + 1])
            o_ref[:, cols] = jnp.where(lane < dh, outs[0], outs[1]).astype(o_ref.dtype)

    pl.when(first < 0)(functools.partial(attend, True))
    pl.when(first >= 0)(functools.partial(attend, False))


def band_attention(q, k, v, bias, batch, seq):
    rows, width = q.shape
    n_groups = seq // BAND_ROWS
    kv_spec = pl.BlockSpec((1, seq, width), lambda b, g: (b, 0, 0))
    return pl.pallas_call(
        _band_attn_kernel,
        grid=(batch, n_groups),
        in_specs=[pl.BlockSpec((BAND_ROWS, width), lambda b, g: (b * n_groups + g, 0)),
                  kv_spec, kv_spec, _resident(bias.shape)],
        out_specs=pl.BlockSpec((BAND_ROWS, width), lambda b, g: (b * n_groups + g, 0)),
        out_shape=jax.ShapeDtypeStruct((rows, width), BF16),
        compiler_params=_params("parallel", "arbitrary"),
        name="band_attn",
    )(q, k.reshape(batch, seq, width), v.reshape(batch, seq, width), bias)


def _fox_cum_kernel(blk, f_ref, bias_ref, tril_ref, o_ref):
    seq = f_ref.shape[1]
    carry = jnp.zeros((1, f_ref.shape[2]), F32)
    for c in range(seq // blk):
        x = f_ref[0, c * blk:(c + 1) * blk, :] + bias_ref[...]
        log_sig = jnp.minimum(x, 0.0) - jnp.log1p(jnp.exp(-jnp.abs(x)))
        cs = jnp.dot(tril_ref[...], log_sig, preferred_element_type=F32,
                     precision=lax.Precision.HIGHEST) + carry
        carry = cs[blk - 1:blk, :]
        o_ref[0, :, c * blk:(c + 1) * blk] = cs.T


def fox_cum(f, f_bias, batch, seq, blk=256):
    width = f.shape[-1]
    bias = jnp.pad(f_bias.astype(F32), (0, width - f_bias.shape[0])).reshape(1, width)
    tril = jnp.asarray(np.tril(np.ones((blk, blk), np.float32)))
    return pl.pallas_call(
        functools.partial(_fox_cum_kernel, blk),
        grid=(batch,),
        in_specs=[pl.BlockSpec((1, seq, width), lambda b: (b, 0, 0)), _resident((1, width)),
                  _resident((blk, blk))],
        out_specs=pl.BlockSpec((1, width, seq), lambda b: (b, 0, 0)),
        out_shape=jax.ShapeDtypeStruct((batch, width, seq), F32),
        compiler_params=_params("parallel"),
        name="fox_cum",
    )(f.reshape(batch, seq, width), bias, tril)


def _fox_attn_kernel(q_ref, k_ref, v_ref, cum_ref, o_ref):
    qi = pl.program_id(2)
    blk, dh = q_ref.shape[1], C_HEAD_DIM
    lane = lax.broadcasted_iota(jnp.int32, (blk, 2 * dh), 1)
    causal = (lax.broadcasted_iota(jnp.int32, (blk, blk), 0)
              >= lax.broadcasted_iota(jnp.int32, (blk, blk), 1))

    def attend(n_past):
        past = n_past * blk
        q_pair = q_ref[0] * jnp.asarray(dh ** -0.5, q_ref.dtype)
        k_all, v_all = k_ref[0, :past + blk, :], v_ref[0, :past + blk, :]
        lane_kv = lax.broadcasted_iota(jnp.int32, v_all.shape, 1)
        outs = []
        for sub in range(2):
            own = (lane < dh) if sub == 0 else (lane >= dh)
            q_h = jnp.where(own, q_pair, jnp.zeros_like(q_pair))
            s = lax.dot_general(q_h, k_all, _NT, preferred_element_type=F32)
            s = s - cum_ref[0, 0, sub:sub + 1, :past + blk]
            parts = [jnp.where(causal, s[:, past:], -jnp.inf)]
            if n_past:
                parts.insert(0, s[:, :past])
            m = functools.reduce(jnp.maximum, [jnp.max(t, axis=-1, keepdims=True) for t in parts])
            p = jnp.concatenate([jnp.exp(t - m).astype(BF16) for t in parts], axis=-1)
            ones_lane = dh if sub == 0 else 0
            own_kv = (lane_kv < dh) if sub == 0 else (lane_kv >= dh)
            v_h = jnp.where(own_kv, v_all,
                            jnp.where(lane_kv == ones_lane, 1.0, 0.0).astype(v_all.dtype))
            o = jnp.dot(p, v_h, preferred_element_type=F32)
            outs.append(o / o[:, ones_lane:ones_lane + 1])
        o_ref[0] = jnp.where(lane < dh, outs[0], outs[1]).astype(o_ref.dtype)

    for n_past in range(k_ref.shape[1] // blk):
        pl.when(qi == n_past)(functools.partial(attend, n_past))


def fox_attention(q, k, v, cum_t, batch, seq, blk=FOX_BLOCK):
    width = q.shape[-1]
    pairs = C_HEADS // 2
    pair_w = 2 * C_HEAD_DIM
    cum = cum_t[:, :C_HEADS, :].reshape(batch, pairs, 2, seq)
    q3, k3, v3 = (t.reshape(batch, seq, width) for t in (q, k, v))
    kv_spec = pl.BlockSpec((1, seq, pair_w), lambda b, p, i: (b, 0, p))
    out = pl.pallas_call(
        _fox_attn_kernel,
        grid=(batch, pairs, seq // blk),
        in_specs=[pl.BlockSpec((1, blk, pair_w), lambda b, p, i: (b, i, p)), kv_spec, kv_spec,
                  pl.BlockSpec((1, 1, 2, seq), lambda b, p, i: (b, p, 0, 0))],
        out_specs=pl.BlockSpec((1, blk, pair_w), lambda b, p, i: (b, i, p)),
        out_shape=jax.ShapeDtypeStruct((batch, seq, width), BF16),
        compiler_params=_params("parallel", "parallel", "arbitrary"),
        name="fox_attn",
    )(q3, k3, v3, cum)
    return out.reshape(batch * seq, width)


def _final_norm_kernel(x_ref, g_ref, o_ref):
    o_ref[...] = _rms_norm(x_ref[...], g_ref[...])


def final_norm(x, gain, row_tile=ROW_TILE):
    rows, d = x.shape
    return pl.pallas_call(
        _final_norm_kernel,
        grid=(rows // row_tile,),
        in_specs=[pl.BlockSpec((row_tile, d), lambda i: (i, 0)), _resident((1, d))],
        out_specs=pl.BlockSpec((row_tile, d), lambda i: (i, 0)),
        out_shape=jax.ShapeDtypeStruct((rows, d), F32),
        compiler_params=_params("parallel"),
        name="final_norm",
    )(x, gain.reshape(1, d))


def _even_mixer(h, gain, w_in, lower_bound, out_gain, rel_table, w_out, batch, seq):
    a_w, b_w = A_HEADS * A_HEAD_DIM, B_HEADS * B_HEAD_DIM
    splits = [(a_w, BF16), (a_w, F32), (a_w, BF16), (a_w, BF16),
              (b_w, BF16), (b_w, BF16), (b_w, BF16)]
    q_a, f_a, i_a, g_a, q_b, k_b, v_b = norm_proj(h, gain, w_in.astype(BF16), splits)
    o_a = hgrn2_mixer(q_a, f_a, i_a, g_a, lower_bound, out_gain, batch, seq)
    o_b = band_attention(q_b, k_b, v_b, rel_bias(rel_table), batch, seq)
    w_out = w_out.astype(BF16)
    return proj_residual(h, [o_a, o_b], [w_out[:a_w], w_out[a_w:]])


def _odd_mixer(h, gain, w_in, f_bias, w_out, batch, seq):
    c_w = C_HEADS * C_HEAD_DIM
    w_pad = jnp.pad(w_in, ((0, 0), (0, LANES - C_HEADS))).astype(BF16)
    splits = [(c_w, BF16), (c_w, BF16), (c_w, BF16), (LANES, F32)]
    q, k, v, f = norm_proj(h, gain, w_pad, splits)
    cum_t = fox_cum(f, f_bias, batch, seq)
    o = fox_attention(q, k, v, cum_t, batch, seq)
    return proj_residual(h, [o], [w_out.astype(BF16)])


def kernel(x, mem, norm_mix, norm_xattn, norm_mem, norm_mlp, norm_final, w_in_ab, a_lb_logits,
           a_out_gain, b_rel_bias, w_out_ab, w_in_c, c_fgate_bias, w_out_c, w_xq, w_xkv, w_xo,
           w_up, w_down):
    batch, seq, d = x.shape
    mem_tokens = mem.shape[1]
    depth = norm_mix.shape[0]
    lb_w = jax.nn.softmax(a_lb_logits.astype(F32), axis=0)
    lower_bounds = jnp.cumsum(lb_w, axis=0) - lb_w[0]
    h = x.reshape(batch * seq, d)
    mem2 = mem.reshape(batch * mem_tokens, d)
    for layer in range(depth):
        if layer % 2 == 0:
            e = layer // 2
            h = _even_mixer(h, norm_mix[layer], w_in_ab[e], lower_bounds[e], a_out_gain[e],
                            b_rel_bias[e], w_out_ab[e], batch, seq)
        else:
            o = layer // 2
            h = _odd_mixer(h, norm_mix[layer], w_in_c[o], c_fgate_bias[o], w_out_c[o], batch, seq)
        k_mem, v_mem = norm_proj(mem2, norm_mem[layer], w_xkv[layer].astype(BF16),
                                 [(d, BF16), (d, BF16)])
        h = xattn_sublayer(h, norm_xattn[layer], w_xq[layer].astype(BF16),
                           k_mem.reshape(batch, mem_tokens, d), v_mem.reshape(batch, mem_tokens, d),
                           w_xo[layer].astype(BF16), seq)
        h = mlp_sublayer(h, norm_mlp[layer], w_up[layer].astype(BF16), w_down[layer].astype(BF16))
    return final_norm(h, norm_final).reshape(batch, seq, d)
```

```python
import functools

import numpy as np
import jax
import jax.numpy as jnp
from jax import lax
from jax.experimental import pallas as pl
from jax.experimental.pallas import tpu as pltpu

F32 = jnp.float32
BF16 = jnp.bfloat16

EPS = 1e-6
CHUNK = 64
A_HEADS = 4
A_HEAD_DIM = 128
B_HEADS = 8
B_HEAD_DIM = 64
B_PREV_CHUNKS = 8
B_BAND = B_PREV_CHUNKS + 1
REL_CLIP = 2 * CHUNK
C_HEADS = 16
C_HEAD_DIM = 64
X_HEADS = 4
HGRN_LEVELS = 6

LANES = 128
VMEM_LIMIT_BYTES = 48 * 1024 * 1024

ROW_TILE = 512
FOX_BLOCK = 256

_NT = (((1,), (1,)), ((), ()))
_TN = (((0,), (0,)), ((), ()))


def _params(*semantics):
    return pltpu.CompilerParams(dimension_semantics=semantics, vmem_limit_bytes=VMEM_LIMIT_BYTES)


def _rms_norm(x, gain):
    return x * lax.rsqrt(jnp.mean(x * x, axis=-1, keepdims=True) + EPS) * gain


def _resident(shape):
    return pl.BlockSpec(shape, lambda *_: (0,) * len(shape))


def _norm_proj_kernel(x_ref, g_ref, w_ref, *out_refs):
    xn = _rms_norm(x_ref[...], g_ref[...]).astype(BF16)
    off = 0
    for o_ref in out_refs:
        width = o_ref.shape[-1]
        o_ref[...] = jnp.dot(xn, w_ref[:, off:off + width],
                             preferred_element_type=F32).astype(o_ref.dtype)
        off += width


def norm_proj(x, gain, w, splits, row_tile=ROW_TILE):
    rows, d = x.shape
    n = w.shape[1]
    assert sum(width for width, _ in splits) == n and rows % row_tile == 0
    return pl.pallas_call(
        _norm_proj_kernel,
        grid=(rows // row_tile,),
        in_specs=[pl.BlockSpec((row_tile, d), lambda i: (i, 0)),
                  _resident((1, d)), _resident((d, n))],
        out_specs=[pl.BlockSpec((row_tile, width), lambda i: (i, 0)) for width, _ in splits],
        out_shape=[jax.ShapeDtypeStruct((rows, width), dt) for width, dt in splits],
        compiler_params=_params("parallel"),
        name="norm_proj",
    )(x, gain.reshape(1, d), w)


def _proj_residual_kernel(n_in, h_ref, *refs):
    a_refs, w_refs, o_ref = refs[:n_in], refs[n_in:2 * n_in], refs[2 * n_in]
    acc = h_ref[...]
    for a_ref, w_ref in zip(a_refs, w_refs):
        acc = acc + jnp.dot(a_ref[...], w_ref[...], preferred_element_type=F32)
    o_ref[...] = acc


def proj_residual(h, acts, weights, row_tile=ROW_TILE):
    rows, d = h.shape
    in_specs = [pl.BlockSpec((row_tile, d), lambda i: (i, 0))]
    in_specs += [pl.BlockSpec((row_tile, a.shape[1]), lambda i: (i, 0)) for a in acts]
    in_specs += [_resident(w.shape) for w in weights]
    return pl.pallas_call(
        functools.partial(_proj_residual_kernel, len(acts)),
        grid=(rows // row_tile,),
        in_specs=in_specs,
        out_specs=pl.BlockSpec((row_tile, d), lambda i: (i, 0)),
        out_shape=jax.ShapeDtypeStruct((rows, d), F32),
        compiler_params=_params("parallel"),
        name="proj_residual",
    )(h, *acts, *weights)


def _mlp_kernel(ff_chunk, h_ref, g_ref, wu_ref, wd_ref, o_ref):
    h = h_ref[...]
    xn = _rms_norm(h, g_ref[...]).astype(BF16)
    acc = h
    for c in range(wu_ref.shape[1] // ff_chunk):
        cols = slice(c * ff_chunk, (c + 1) * ff_chunk)
        u = jnp.maximum(jnp.dot(xn, wu_ref[:, cols], preferred_element_type=F32), 0.0)
        acc = acc + jnp.dot((u * u).astype(BF16), wd_ref[cols, :], preferred_element_type=F32)
    o_ref[...] = acc


def mlp_sublayer(h, gain, w_up, w_down, row_tile=ROW_TILE, ff_chunk=1024):
    rows, d = h.shape
    return pl.pallas_call(
        functools.partial(_mlp_kernel, ff_chunk),
        grid=(rows // row_tile,),
        in_specs=[pl.BlockSpec((row_tile, d), lambda i: (i, 0)), _resident((1, d)),
                  _resident(w_up.shape), _resident(w_down.shape)],
        out_specs=pl.BlockSpec((row_tile, d), lambda i: (i, 0)),
        out_shape=jax.ShapeDtypeStruct((rows, d), F32),
        compiler_params=_params("parallel"),
        name="mlp",
    )(h, gain.reshape(1, d), w_up, w_down)


def _xattn_kernel(h_ref, g_ref, wq_ref, k_ref, v_ref, wo_ref, o_ref):
    h = h_ref[...]
    d = h.shape[-1]
    dh = d // X_HEADS
    xn = _rms_norm(h, g_ref[...]).astype(BF16)
    q = (jnp.dot(xn, wq_ref[...], preferred_element_type=F32) * dh ** -0.5).astype(BF16)
    outs = []
    for hd in range(X_HEADS):
        cols = slice(hd * dh, (hd + 1) * dh)
        s = lax.dot_general(q[:, cols], k_ref[0, :, cols], _NT, preferred_element_type=F32)
        p = jnp.exp(s - jnp.max(s, axis=-1, keepdims=True))
        denom = jnp.sum(p, axis=-1, keepdims=True)
        o = jnp.dot(p.astype(BF16), v_ref[0, :, cols], preferred_element_type=F32)
        outs.append((o / denom).astype(BF16))
    o_all = jnp.concatenate(outs, axis=-1)
    o_ref[...] = h + jnp.dot(o_all, wo_ref[...], preferred_element_type=F32)


def xattn_sublayer(h, gain, w_q, k_mem, v_mem, w_o, seq, row_tile=ROW_TILE):
    rows, d = h.shape
    tiles_per_seq = seq // row_tile
    mem_block = (1,) + k_mem.shape[1:]
    return pl.pallas_call(
        _xattn_kernel,
        grid=(rows // row_tile,),
        in_specs=[pl.BlockSpec((row_tile, d), lambda i: (i, 0)), _resident((1, d)),
                  _resident(w_q.shape),
                  pl.BlockSpec(mem_block, lambda i: (i // tiles_per_seq, 0, 0)),
                  pl.BlockSpec(mem_block, lambda i: (i // tiles_per_seq, 0, 0)),
                  _resident(w_o.shape)],
        out_specs=pl.BlockSpec((row_tile, d), lambda i: (i, 0)),
        out_shape=jax.ShapeDtypeStruct((rows, d), F32),
        compiler_params=_params("parallel"),
        name="xattn",
    )(h, gain.reshape(1, d), w_q, k_mem, v_mem, w_o)


def _hgrn_tables():
    c = CHUNK
    t = np.arange(c)[:, None]
    j = np.arange(c)[None, :]
    blocks = [(j <= t), (j > t)]
    masks = [np.eye(c, dtype=bool)]
    for level in range(HGRN_LEVELS):
        size = c >> level
        half = size // 2
        ref = (t // size) * size + half - 1
        upper = (t % size) >= half
        blocks.append(np.where(upper, (j > ref) & (j <= t), (j > t) & (j <= ref)))
        tt, ss = np.arange(c)[:, None], np.arange(c)[None, :]
        masks.append((tt // size == ss // size) & ((tt % size) >= half) & ((ss % size) < half))
    w = np.concatenate(blocks, axis=0).astype(np.float32)
    return np.concatenate([w, w], axis=1), np.stack(masks).astype(np.float32)


def _hgrn_kernel(q_ref, f_ref, i_ref, g_ref, lb_ref, gain_ref, ww_ref, mask_ref, o_ref, st_ref):
    c, dk = CHUNK, A_HEAD_DIM

    @pl.when(pl.program_id(1) == 0)
    def _():
        st_ref[...] = jnp.zeros_like(st_ref)

    z = f_ref[...]
    lb = lb_ref[...]
    log_f = jnp.log(lb + (1.0 - lb) * jax.nn.sigmoid(z))
    kk = (1.0 - lb) * jax.nn.sigmoid(-z)
    qf = q_ref[...].astype(F32)
    qs = qf * jax.nn.sigmoid(qf) * dk ** -0.5
    hi = log_f.astype(BF16)
    lo = (log_f - hi.astype(F32)).astype(BF16)
    row = lax.broadcasted_iota(jnp.int32, (c, q_ref.shape[1]), 0)

    head_cols = [slice(hd * dk, (hd + 1) * dk) for hd in range(A_HEADS)]
    n_chunks = q_ref.shape[0] // c

    independent = []
    for ci in range(n_chunks):
        rows = slice(ci * c, (ci + 1) * c)
        x = jnp.dot(ww_ref[...], jnp.concatenate([hi[rows], lo[rows]], axis=0),
                    preferred_element_type=F32)
        b = x[0:c]
        chunk_decay = jnp.exp(b[c - 1:c, :])
        q_c, k_c, v = qs[rows], kk[rows], i_ref[rows, :]
        q_in = (q_c * jnp.exp(b)).astype(BF16)
        k_out = (k_c * jnp.exp(x[c:2 * c])).astype(BF16)
        sides = [(q_c.astype(BF16), k_c.astype(BF16))]
        for level in range(HGRN_LEVELS):
            half = (c >> level) // 2
            e = jnp.exp(x[(2 + level) * c:(3 + level) * c])
            m = (jnp.where((row & half) != 0, q_c, k_c) * e).astype(BF16)
            sides.append((m, m))
        o_intra, upd_t = [], []
        for cols in head_cols:
            a = sum(lax.dot_general(lhs[:, cols], rhs[:, cols], _NT,
                                    preferred_element_type=F32) * mask_ref[n]
                    for n, (lhs, rhs) in enumerate(sides))
            o_intra.append(jnp.dot(a.astype(BF16), v[:, cols], preferred_element_type=F32))
            upd_t.append(lax.dot_general(v[:, cols], k_out[:, cols], _TN,
                                         preferred_element_type=F32))
        independent.append((q_in, chunk_decay, o_intra, upd_t))

    states = [st_ref[hd] for hd in range(A_HEADS)]
    for ci, (q_in, chunk_decay, o_intra, upd_t) in enumerate(independent):
        outs = []
        for hd, cols in enumerate(head_cols):
            o = o_intra[hd] + lax.dot_general(q_in[:, cols], states[hd].astype(BF16), _NT,
                                              preferred_element_type=F32)
            states[hd] = states[hd] * chunk_decay[:, cols] + upd_t[hd]
            outs.append(_rms_norm(o, gain_ref[:, cols]))
        rows = slice(ci * c, (ci + 1) * c)
        gf = g_ref[rows, :].astype(F32)
        o_ref[rows, :] = (jnp.concatenate(outs, axis=-1)
                          * (gf * jax.nn.sigmoid(gf))).astype(o_ref.dtype)
    for hd in range(A_HEADS):
        st_ref[hd] = states[hd]


HGRN_GROUP = 4


def hgrn2_mixer(q, f, i, g, lower_bound, out_gain, batch, seq):
    rows, width = q.shape
    n_chunks = seq // (HGRN_GROUP * CHUNK)
    ww, masks = _hgrn_tables()
    act = pl.BlockSpec((HGRN_GROUP * CHUNK, width), lambda b, n: (b * n_chunks + n, 0))
    return pl.pallas_call(
        _hgrn_kernel,
        grid=(batch, n_chunks),
        in_specs=[act, act, act, act, _resident((1, width)), _resident((1, width)),
                  _resident(ww.shape), _resident(masks.shape)],
        out_specs=act,
        out_shape=jax.ShapeDtypeStruct((rows, width), BF16),
        scratch_shapes=[pltpu.VMEM((A_HEADS, A_HEAD_DIM, A_HEAD_DIM), F32)],
        compiler_params=_params("parallel", "arbitrary"),
        name="hgrn2",
    )(q, f, i, g, lower_bound.reshape(1, width), out_gain.reshape(1, width),
      jnp.asarray(ww, BF16), jnp.asarray(masks))


BAND_GROUP = 4
BAND_ROWS = BAND_GROUP * CHUNK
BAND_WINDOW = (BAND_GROUP + B_PREV_CHUNKS) * CHUNK
BIAS_SPAN = BAND_ROWS + BAND_WINDOW


def _rel_bias_kernel(table_ref, o_ref):
    heads, n_idx = table_ref.shape
    e = lax.broadcasted_iota(jnp.int32, (n_idx, BIAS_SPAN), 1)
    distance = B_PREV_CHUNKS * CHUNK + BAND_ROWS - 1 - e
    idx = jnp.clip(distance, -REL_CLIP, REL_CLIP) + REL_CLIP
    onehot = (lax.broadcasted_iota(jnp.int32, (n_idx, BIAS_SPAN), 0) == idx).astype(F32)
    by_offset = jnp.dot(table_ref[...], onehot, preferred_element_type=F32,
                        precision=lax.Precision.HIGHEST)
    col = lax.broadcasted_iota(jnp.int32, (heads, BAND_WINDOW), 1)
    for r in range(BAND_ROWS):
        band_start = (r // CHUNK) * CHUNK
        in_band = (col >= band_start) & (col < band_start + B_BAND * CHUNK)
        shift = BAND_ROWS - 1 - r
        o_ref[:, r * BAND_WINDOW:(r + 1) * BAND_WINDOW] = jnp.where(
            in_band, by_offset[:, shift:shift + BAND_WINDOW], -jnp.inf)


def rel_bias(rel_table):
    heads, n_rel = rel_table.shape
    n_idx = -(-n_rel // LANES) * LANES
    table = jnp.pad(rel_table.astype(F32), ((0, 0), (0, n_idx - n_rel)))
    out = pl.pallas_call(
        _rel_bias_kernel,
        in_specs=[pl.BlockSpec(memory_space=pltpu.VMEM)],
        out_specs=pl.BlockSpec(memory_space=pltpu.VMEM),
        out_shape=jax.ShapeDtypeStruct((heads, BAND_ROWS * BAND_WINDOW), F32),
        compiler_params=pltpu.CompilerParams(vmem_limit_bytes=VMEM_LIMIT_BYTES),
        name="rel_bias",
    )(table)
    return out.reshape(heads, BAND_ROWS, BAND_WINDOW)


def _band_attn_kernel(q_ref, k_ref, v_ref, bias_ref, o_ref):
    g = pl.program_id(1)
    c, dh = CHUNK, B_HEAD_DIM
    n_slots = BAND_WINDOW // c
    first = g * BAND_GROUP - B_PREV_CHUNKS
    lane = lax.broadcasted_iota(jnp.int32, (BAND_ROWS, 2 * dh), 1)
    lane_kv = lax.broadcasted_iota(jnp.int32, (BAND_WINDOW, 2 * dh), 1)

    def attend(at_sequence_start):
        starts = [pl.multiple_of(jnp.maximum(first + j, 0) * c, c) for j in range(n_slots)]
        k_win = jnp.concatenate([k_ref[0, pl.ds(s, c), :] for s in starts], axis=0)
        v_win = jnp.concatenate([v_ref[0, pl.ds(s, c), :] for s in starts], axis=0)
        started = lax.broadcasted_iota(jnp.int32, (BAND_ROWS, BAND_WINDOW), 1) >= -first * c
        q_all = q_ref[...] * jnp.asarray(dh ** -0.5, q_ref.dtype)
        for pair in range(B_HEADS // 2):
            cols = slice(pair * 2 * dh, (pair + 1) * 2 * dh)
            q_pair, k_pair, v_pair = q_all[:, cols], k_win[:, cols], v_win[:, cols]
            q_both = jnp.concatenate([jnp.where(lane < dh, q_pair, jnp.zeros_like(q_pair)),
                                      jnp.where(lane >= dh, q_pair, jnp.zeros_like(q_pair))],
                                     axis=0)
            s_both = lax.dot_general(q_both, k_pair, _NT, preferred_element_type=F32)
            outs = []
            for sub in range(2):
                s = s_both[sub * BAND_ROWS:(sub + 1) * BAND_ROWS] + bias_ref[2 * pair + sub]
                if at_sequence_start:
                    s = jnp.where(started, s, -jnp.inf)
                p = jnp.exp(s - jnp.max(s, axis=-1, keepdims=True)).astype(BF16)
                ones_lane = dh if sub == 0 else 0
                own_kv = (lane_kv < dh) if sub == 0 else (lane_kv >= dh)
                v_h = jnp.where(own_kv, v_pair,
                                jnp.where(lane_kv == ones_lane, 1.0, 0.0).astype(v_pair.dtype))
                o = jnp.dot(p, v_h, preferred_element_type=F32)
                outs.append(o / o[:, ones_lane:ones_lane + 1])
            o_ref[:, cols] = jnp.where(lane < dh, outs[0], outs[1]).astype(o_ref.dtype)

    pl.when(first < 0)(functools.partial(attend, True))
    pl.when(first >= 0)(functools.partial(attend, False))


def band_attention(q, k, v, bias, batch, seq):
    rows, width = q.shape
    n_groups = seq // BAND_ROWS
    kv_spec = pl.BlockSpec((1, seq, width), lambda b, g: (b, 0, 0))
    return pl.pallas_call(
        _band_attn_kernel,
        grid=(batch, n_groups),
        in_specs=[pl.BlockSpec((BAND_ROWS, width), lambda b, g: (b * n_groups + g, 0)),
                  kv_spec, kv_spec, _resident(bias.shape)],
        out_specs=pl.BlockSpec((BAND_ROWS, width), lambda b, g: (b * n_groups + g, 0)),
        out_shape=jax.ShapeDtypeStruct((rows, width), BF16),
        compiler_params=_params("parallel", "arbitrary"),
        name="band_attn",
    )(q, k.reshape(batch, seq, width), v.reshape(batch, seq, width), bias)


def _fox_cum_kernel(blk, f_ref, bias_ref, tril_ref, o_ref):
    seq = f_ref.shape[1]
    carry = jnp.zeros((1, f_ref.shape[2]), F32)
    for c in range(seq // blk):
        x = f_ref[0, c * blk:(c + 1) * blk, :] + bias_ref[...]
        log_sig = jnp.minimum(x, 0.0) - jnp.log1p(jnp.exp(-jnp.abs(x)))
        cs = jnp.dot(tril_ref[...], log_sig, preferred_element_type=F32,
                     precision=lax.Precision.HIGHEST) + carry
        carry = cs[blk - 1:blk, :]
        o_ref[0, :, c * blk:(c + 1) * blk] = cs.T


def fox_cum(f, f_bias, batch, seq, blk=256):
    width = f.shape[-1]
    bias = jnp.pad(f_bias.astype(F32), (0, width - f_bias.shape[0])).reshape(1, width)
    tril = jnp.asarray(np.tril(np.ones((blk, blk), np.float32)))
    return pl.pallas_call(
        functools.partial(_fox_cum_kernel, blk),
        grid=(batch,),
        in_specs=[pl.BlockSpec((1, seq, width), lambda b: (b, 0, 0)), _resident((1, width)),
                  _resident((blk, blk))],
        out_specs=pl.BlockSpec((1, width, seq), lambda b: (b, 0, 0)),
        out_shape=jax.ShapeDtypeStruct((batch, width, seq), F32),
        compiler_params=_params("parallel"),
        name="fox_cum",
    )(f.reshape(batch, seq, width), bias, tril)


def _fox_attn_kernel(q_ref, k_ref, v_ref, cum_ref, o_ref):
    qi = pl.program_id(2)
    blk, dh = q_ref.shape[1], C_HEAD_DIM
    lane = lax.broadcasted_iota(jnp.int32, (blk, 2 * dh), 1)
    causal = (lax.broadcasted_iota(jnp.int32, (blk, blk), 0)
              >= lax.broadcasted_iota(jnp.int32, (blk, blk), 1))

    def attend(n_past):
        past = n_past * blk
        lane_kv = lax.broadcasted_iota(jnp.int32, (past + blk, 2 * dh), 1)
        for pair in range(q_ref.shape[2] // (2 * dh)):
            cols = slice(pair * 2 * dh, (pair + 1) * 2 * dh)
            q_pair = q_ref[0, :, cols] * jnp.asarray(dh ** -0.5, q_ref.dtype)
            k_all, v_all = k_ref[0, :past + blk, cols], v_ref[0, :past + blk, cols]
            q_both = jnp.concatenate([jnp.where(lane < dh, q_pair, jnp.zeros_like(q_pair)),
                                      jnp.where(lane >= dh, q_pair, jnp.zeros_like(q_pair))],
                                     axis=0)
            s_both = lax.dot_general(q_both, k_all, _NT, preferred_element_type=F32)
            outs = []
            for sub in range(2):
                s = s_both[sub * blk:(sub + 1) * blk] - cum_ref[0, pair, sub:sub + 1, :past + blk]
                parts = [jnp.where(causal, s[:, past:], -jnp.inf)]
                if n_past:
                    parts.insert(0, s[:, :past])
                m = functools.reduce(jnp.maximum,
                                     [jnp.max(t, axis=-1, keepdims=True) for t in parts])
                p = jnp.concatenate([jnp.exp(t - m).astype(BF16) for t in parts], axis=-1)
                ones_lane = dh if sub == 0 else 0
                own_kv = (lane_kv < dh) if sub == 0 else (lane_kv >= dh)
                v_h = jnp.where(own_kv, v_all,
                                jnp.where(lane_kv == ones_lane, 1.0, 0.0).astype(v_all.dtype))
                o = jnp.dot(p, v_h, preferred_element_type=F32)
                outs.append(o / o[:, ones_lane:ones_lane + 1])
            o_ref[0, :, cols] = jnp.where(lane < dh, outs[0], outs[1]).astype(o_ref.dtype)

    for n_past in range(k_ref.shape[1] // blk):
        pl.when(qi == n_past)(functools.partial(attend, n_past))


FOX_PAIRS_PER_STEP = 2


def fox_attention(q, k, v, cum_t, batch, seq, blk=FOX_BLOCK):
    width = q.shape[-1]
    pairs = C_HEADS // 2 // FOX_PAIRS_PER_STEP
    pair_w = 2 * C_HEAD_DIM * FOX_PAIRS_PER_STEP
    cum = cum_t[:, :C_HEADS, :].reshape(batch, C_HEADS // 2, 2, seq)
    q3, k3, v3 = (t.reshape(batch, seq, width) for t in (q, k, v))
    kv_spec = pl.BlockSpec((1, seq, pair_w), lambda b, p, i: (b, 0, p))
    out = pl.pallas_call(
        _fox_attn_kernel,
        grid=(batch, pairs, seq // blk),
        in_specs=[pl.BlockSpec((1, blk, pair_w), lambda b, p, i: (b, i, p)), kv_spec, kv_spec,
                  pl.BlockSpec((1, FOX_PAIRS_PER_STEP, 2, seq), lambda b, p, i: (b, p, 0, 0))],
        out_specs=pl.BlockSpec((1, blk, pair_w), lambda b, p, i: (b, i, p)),
        out_shape=jax.ShapeDtypeStruct((batch, seq, width), BF16),
        compiler_params=_params("parallel", "parallel", "arbitrary"),
        name="fox_attn",
    )(q3, k3, v3, cum)
    return out.reshape(batch * seq, width)


def _final_norm_kernel(x_ref, g_ref, o_ref):
    o_ref[...] = _rms_norm(x_ref[...], g_ref[...])


def final_norm(x, gain, row_tile=ROW_TILE):
    rows, d = x.shape
    return pl.pallas_call(
        _final_norm_kernel,
        grid=(rows // row_tile,),
        in_specs=[pl.BlockSpec((row_tile, d), lambda i: (i, 0)), _resident((1, d))],
        out_specs=pl.BlockSpec((row_tile, d), lambda i: (i, 0)),
        out_shape=jax.ShapeDtypeStruct((rows, d), F32),
        compiler_params=_params("parallel"),
        name="final_norm",
    )(x, gain.reshape(1, d))


def _even_mixer(h, gain, w_in, lower_bound, out_gain, rel_table, w_out, batch, seq):
    a_w, b_w = A_HEADS * A_HEAD_DIM, B_HEADS * B_HEAD_DIM
    splits = [(a_w, BF16), (a_w, F32), (a_w, BF16), (a_w, BF16),
              (b_w, BF16), (b_w, BF16), (b_w, BF16)]
    q_a, f_a, i_a, g_a, q_b, k_b, v_b = norm_proj(h, gain, w_in.astype(BF16), splits)
    o_a = hgrn2_mixer(q_a, f_a, i_a, g_a, lower_bound, out_gain, batch, seq)
    o_b = band_attention(q_b, k_b, v_b, rel_bias(rel_table), batch, seq)
    w_out = w_out.astype(BF16)
    return proj_residual(h, [o_a, o_b], [w_out[:a_w], w_out[a_w:]])


def _odd_mixer(h, gain, w_in, f_bias, w_out, batch, seq):
    c_w = C_HEADS * C_HEAD_DIM
    w_pad = jnp.pad(w_in, ((0, 0), (0, LANES - C_HEADS))).astype(BF16)
    splits = [(c_w, BF16), (c_w, BF16), (c_w, BF16), (LANES, F32)]
    q, k, v, f = norm_proj(h, gain, w_pad, splits)
    cum_t = fox_cum(f, f_bias, batch, seq)
    o = fox_attention(q, k, v, cum_t, batch, seq)
    return proj_residual(h, [o], [w_out.astype(BF16)])


def kernel(x, mem, norm_mix, norm_xattn, norm_mem, norm_mlp, norm_final, w_in_ab, a_lb_logits,
           a_out_gain, b_rel_bias, w_out_ab, w_in_c, c_fgate_bias, w_out_c, w_xq, w_xkv, w_xo,
           w_up, w_down):
    batch, seq, d = x.shape
    mem_tokens = mem.shape[1]
    depth = norm_mix.shape[0]
    lb_w = jax.nn.softmax(a_lb_logits.astype(F32), axis=0)
    lower_bounds = jnp.cumsum(lb_w, axis=0) - lb_w[0]
    h = x.reshape(batch * seq, d)
    mem2 = mem.reshape(batch * mem_tokens, d)
    for layer in range(depth):
        if layer % 2 == 0:
            e = layer // 2
            h = _even_mixer(h, norm_mix[layer], w_in_ab[e], lower_bounds[e], a_out_gain[e],
                            b_rel_bias[e], w_out_ab[e], batch, seq)
        else:
            o = layer // 2
            h = _odd_mixer(h, norm_mix[layer], w_in_c[o], c_fgate_bias[o], w_out_c[o], batch, seq)
        k_mem, v_mem = norm_proj(mem2, norm_mem[layer], w_xkv[layer].astype(BF16),
                                 [(d, BF16), (d, BF16)])
        h = xattn_sublayer(h, norm_xattn[layer], w_xq[layer].astype(BF16),
                           k_mem.reshape(batch, mem_tokens, d), v_mem.reshape(batch, mem_tokens, d),
                           w_xo[layer].astype(BF16), seq)
        h = mlp_sublayer(h, norm_mlp[layer], w_up[layer].astype(BF16), w_down[layer].astype(BF16))
    return final_norm(h, norm_final).reshape(batch, seq, d)
```

```python
import functools

import numpy as np
import jax
import jax.numpy as jnp
from jax import lax
from jax.experimental import pallas as pl
from jax.experimental.pallas import tpu as pltpu

F32 = jnp.float32
BF16 = jnp.bfloat16

EPS = 1e-6
LOG2_E = 1.4426950408889634
CHUNK = 64
A_HEADS = 4
A_HEAD_DIM = 128
B_HEADS = 8
B_HEAD_DIM = 64
B_PREV_CHUNKS = 8
B_BAND = B_PREV_CHUNKS + 1
REL_CLIP = 2 * CHUNK
C_HEADS = 16
C_HEAD_DIM = 64
X_HEADS = 4
HGRN_LEVELS = 6

LANES = 128
VMEM_LIMIT_BYTES = 48 * 1024 * 1024
TAIL_VMEM_LIMIT_BYTES = 56 * 1024 * 1024

ROW_TILE = 512
FOX_BLOCK = 256

_NT = (((1,), (1,)), ((), ()))
_TN = (((0,), (0,)), ((), ()))


def _params(*semantics):
    return pltpu.CompilerParams(dimension_semantics=semantics, vmem_limit_bytes=VMEM_LIMIT_BYTES)


def _rms_norm(x, gain):
    return x * lax.rsqrt(jnp.mean(x * x, axis=-1, keepdims=True) + EPS) * gain


def _resident(shape):
    return pl.BlockSpec(shape, lambda *_: (0,) * len(shape), pipeline_mode=pl.Buffered(1))


def _norm_proj_kernel(scales, x_ref, g_ref, w_ref, *out_refs):
    xn = _rms_norm(x_ref[...], g_ref[...]).astype(BF16)
    off = 0
    for o_ref, scale in zip(out_refs, scales):
        width = o_ref.shape[-1]
        y = jnp.dot(xn, w_ref[:, off:off + width], preferred_element_type=F32)
        o_ref[...] = (y if scale == 1.0 else y * scale).astype(o_ref.dtype)
        off += width


def norm_proj(x, gain, w, splits, row_tile=ROW_TILE):
    rows, d = x.shape
    n = w.shape[1]
    assert sum(width for width, _, _ in splits) == n and rows % row_tile == 0
    return pl.pallas_call(
        functools.partial(_norm_proj_kernel, tuple(scale for _, _, scale in splits)),
        grid=(rows // row_tile,),
        in_specs=[pl.BlockSpec((row_tile, d), lambda i: (i, 0)),
                  _resident((1, d)), _resident((d, n))],
        out_specs=[pl.BlockSpec((row_tile, width), lambda i: (i, 0)) for width, _, _ in splits],
        out_shape=[jax.ShapeDtypeStruct((rows, width), dt) for width, dt, _ in splits],
        compiler_params=_params("parallel"),
        name="norm_proj",
    )(x, gain.reshape(1, d), w)


def _xattn_update(h, gain, wq_ref, k_ref, v_ref, wo_ref):
    dh = h.shape[-1] // X_HEADS
    xn = _rms_norm(h, gain).astype(BF16)
    q = (jnp.dot(xn, wq_ref[...], preferred_element_type=F32) * (dh ** -0.5 * LOG2_E)).astype(BF16)
    outs = []
    for hd in range(X_HEADS):
        cols = slice(hd * dh, (hd + 1) * dh)
        s = lax.dot_general(q[:, cols], k_ref[0, :, cols], _NT, preferred_element_type=F32)
        p = jnp.exp2(s - jnp.max(s, axis=-1, keepdims=True))
        denom = jnp.sum(p, axis=-1, keepdims=True)
        o = jnp.dot(p.astype(BF16), v_ref[0, :, cols], preferred_element_type=F32)
        outs.append((o / denom).astype(BF16))
    return h + jnp.dot(jnp.concatenate(outs, axis=-1), wo_ref[...], preferred_element_type=F32)


def _mlp_update(h, gain, wu_ref, wd_ref, ff_chunk):
    xn = _rms_norm(h, gain).astype(BF16)
    acc = h
    for c in range(wu_ref.shape[1] // ff_chunk):
        cols = slice(c * ff_chunk, (c + 1) * ff_chunk)
        u = jnp.maximum(jnp.dot(xn, wu_ref[:, cols], preferred_element_type=F32), 0.0)
        acc = acc + jnp.dot((u * u).astype(BF16), wd_ref[cols, :], preferred_element_type=F32)
    return acc


def _layer_tail_kernel(n_act, final, ff_chunk, h_ref, *refs):
    a_refs, w_refs = refs[:n_act], refs[n_act:2 * n_act]
    gx_ref, wq_ref, k_ref, v_ref, wo_ref, gm_ref, wu_ref, wd_ref = refs[2 * n_act:2 * n_act + 8]
    o_ref = refs[-1]
    h = h_ref[...]
    for a_ref, w_ref in zip(a_refs, w_refs):
        h = h + jnp.dot(a_ref[...], w_ref[...], preferred_element_type=F32)
    h = _xattn_update(h, gx_ref[...], wq_ref, k_ref, v_ref, wo_ref)
    h = _mlp_update(h, gm_ref[...], wu_ref, wd_ref, ff_chunk)
    if final:
        h = _rms_norm(h, refs[-2][...])
    o_ref[...] = h


def layer_tail(h, acts, w_outs, g_xattn, w_q, k_mem, v_mem, w_o, g_mlp, w_up, w_down, seq,
               g_final=None, row_tile=ROW_TILE, ff_chunk=1024):
    rows, d = h.shape
    tiles_per_seq = seq // row_tile
    row_spec = lambda width: pl.BlockSpec((row_tile, width), lambda i: (i, 0))
    mem_spec = pl.BlockSpec((1,) + k_mem.shape[1:], lambda i: (i // tiles_per_seq, 0, 0))
    gains = [g.reshape(1, d) for g in (g_xattn, g_mlp)]
    operands = [h, *acts, *w_outs, gains[0], w_q, k_mem, v_mem, w_o, gains[1], w_up, w_down]
    in_specs = ([row_spec(d)] + [row_spec(a.shape[1]) for a in acts]
                + [_resident(w.shape) for w in w_outs]
                + [_resident((1, d)), _resident(w_q.shape), mem_spec, mem_spec,
                   _resident(w_o.shape), _resident((1, d)), _resident(w_up.shape),
                   _resident(w_down.shape)])
    if g_final is not None:
        operands.append(g_final.reshape(1, d))
        in_specs.append(_resident((1, d)))
    return pl.pallas_call(
        functools.partial(_layer_tail_kernel, len(acts), g_final is not None, ff_chunk),
        grid=(rows // row_tile,),
        in_specs=in_specs,
        out_specs=row_spec(d),
        out_shape=jax.ShapeDtypeStruct((rows, d), F32),
        compiler_params=pltpu.CompilerParams(dimension_semantics=("parallel",),
                                             vmem_limit_bytes=TAIL_VMEM_LIMIT_BYTES),
        name="layer_tail",
    )(*operands)


def _hgrn_tables():
    c = CHUNK
    t = np.arange(c)[:, None]
    j = np.arange(c)[None, :]
    blocks = [(j <= t), (j > t)]
    masks = [np.eye(c, dtype=bool)]
    for level in range(HGRN_LEVELS):
        size = c >> level
        half = size // 2
        ref = (t // size) * size + half - 1
        upper = (t % size) >= half
        blocks.append(np.where(upper, (j > ref) & (j <= t), (j > t) & (j <= ref)))
        tt, ss = np.arange(c)[:, None], np.arange(c)[None, :]
        masks.append((tt // size == ss // size) & ((tt % size) >= half) & ((ss % size) < half))
    w = np.concatenate(blocks, axis=0).astype(np.float32)
    return np.concatenate([w, w], axis=1), np.stack(masks).astype(np.float32)


def _hgrn_kernel(q_ref, f_ref, i_ref, g_ref, lb_ref, gain_ref, ww_ref, mask_ref, o_ref, st_ref):
    c, dk = CHUNK, A_HEAD_DIM

    @pl.when(pl.program_id(1) == 0)
    def _():
        st_ref[...] = jnp.zeros_like(st_ref)

    z = f_ref[...]
    lb = lb_ref[...]
    log_f = jnp.log(lb + (1.0 - lb) * jax.nn.sigmoid(z))
    kk = (1.0 - lb) * jax.nn.sigmoid(-z)
    qf = q_ref[...].astype(F32)
    qs = qf * jax.nn.sigmoid(qf) * dk ** -0.5
    hi = log_f.astype(BF16)
    lo = (log_f - hi.astype(F32)).astype(BF16)
    row = lax.broadcasted_iota(jnp.int32, (c, q_ref.shape[1]), 0)

    head_cols = [slice(hd * dk, (hd + 1) * dk) for hd in range(A_HEADS)]
    n_chunks = q_ref.shape[0] // c

    independent = []
    for ci in range(n_chunks):
        rows = slice(ci * c, (ci + 1) * c)
        x = jnp.dot(ww_ref[...], jnp.concatenate([hi[rows], lo[rows]], axis=0),
                    preferred_element_type=F32)
        b = x[0:c]
        chunk_decay = jnp.exp(b[c - 1:c, :])
        q_c, k_c, v = qs[rows], kk[rows], i_ref[rows, :]
        q_in = (q_c * jnp.exp(b)).astype(BF16)
        k_out = (k_c * jnp.exp(x[c:2 * c])).astype(BF16)
        sides = [(q_c.astype(BF16), k_c.astype(BF16))]
        for level in range(HGRN_LEVELS):
            half = (c >> level) // 2
            e = jnp.exp(x[(2 + level) * c:(3 + level) * c])
            m = (jnp.where((row & half) != 0, q_c, k_c) * e).astype(BF16)
            sides.append((m, m))
        o_intra, upd_t = [], []
        for cols in head_cols:
            a = sum(lax.dot_general(lhs[:, cols], rhs[:, cols], _NT,
                                    preferred_element_type=F32) * mask_ref[n]
                    for n, (lhs, rhs) in enumerate(sides))
            o_intra.append(jnp.dot(a.astype(BF16), v[:, cols], preferred_element_type=F32))
            upd_t.append(lax.dot_general(v[:, cols], k_out[:, cols], _TN,
                                         preferred_element_type=F32))
        independent.append((q_in, chunk_decay, o_intra, upd_t))

    states = [st_ref[hd] for hd in range(A_HEADS)]
    for ci, (q_in, chunk_decay, o_intra, upd_t) in enumerate(independent):
        outs = []
        for hd, cols in enumerate(head_cols):
            o = o_intra[hd] + lax.dot_general(q_in[:, cols], states[hd].astype(BF16), _NT,
                                              preferred_element_type=F32)
            states[hd] = states[hd] * chunk_decay[:, cols] + upd_t[hd]
            outs.append(_rms_norm(o, gain_ref[:, cols]))
        rows = slice(ci * c, (ci + 1) * c)
        gf = g_ref[rows, :].astype(F32)
        o_ref[rows, :] = (jnp.concatenate(outs, axis=-1)
                          * (gf * jax.nn.sigmoid(gf))).astype(o_ref.dtype)
    for hd in range(A_HEADS):
        st_ref[hd] = states[hd]


HGRN_GROUP = 4


def hgrn2_mixer(q, f, i, g, lower_bound, out_gain, batch, seq):
    rows, width = q.shape
    n_chunks = seq // (HGRN_GROUP * CHUNK)
    ww, masks = _hgrn_tables()
    act = pl.BlockSpec((HGRN_GROUP * CHUNK, width), lambda b, n: (b * n_chunks + n, 0))
    return pl.pallas_call(
        _hgrn_kernel,
        grid=(batch, n_chunks),
        in_specs=[act, act, act, act, _resident((1, width)), _resident((1, width)),
                  _resident(ww.shape), _resident(masks.shape)],
        out_specs=act,
        out_shape=jax.ShapeDtypeStruct((rows, width), BF16),
        scratch_shapes=[pltpu.VMEM((A_HEADS, A_HEAD_DIM, A_HEAD_DIM), F32)],
        compiler_params=_params("parallel", "arbitrary"),
        name="hgrn2",
    )(q, f, i, g, lower_bound.reshape(1, width), out_gain.reshape(1, width),
      jnp.asarray(ww, BF16), jnp.asarray(masks))


BAND_GROUP = 4
BAND_ROWS = BAND_GROUP * CHUNK
BAND_WINDOW = (BAND_GROUP + B_PREV_CHUNKS) * CHUNK
BIAS_SPAN = BAND_ROWS + BAND_WINDOW


def _rel_bias_kernel(table_ref, o_ref):
    heads, n_idx = table_ref.shape
    e = lax.broadcasted_iota(jnp.int32, (n_idx, BIAS_SPAN), 1)
    distance = B_PREV_CHUNKS * CHUNK + BAND_ROWS - 1 - e
    idx = jnp.clip(distance, -REL_CLIP, REL_CLIP) + REL_CLIP
    onehot = (lax.broadcasted_iota(jnp.int32, (n_idx, BIAS_SPAN), 0) == idx).astype(F32)
    by_offset = jnp.dot(table_ref[...], onehot, preferred_element_type=F32,
                        precision=lax.Precision.HIGHEST) * LOG2_E
    col = lax.broadcasted_iota(jnp.int32, (heads, BAND_WINDOW), 1)
    for r in range(BAND_ROWS):
        band_start = (r // CHUNK) * CHUNK
        in_band = (col >= band_start) & (col < band_start + B_BAND * CHUNK)
        shift = BAND_ROWS - 1 - r
        o_ref[:, r * BAND_WINDOW:(r + 1) * BAND_WINDOW] = jnp.where(
            in_band, by_offset[:, shift:shift + BAND_WINDOW], -jnp.inf)


def rel_bias(rel_table):
    heads, n_rel = rel_table.shape
    n_idx = -(-n_rel // LANES) * LANES
    table = jnp.pad(rel_table.astype(F32), ((0, 0), (0, n_idx - n_rel)))
    out = pl.pallas_call(
        _rel_bias_kernel,
        in_specs=[pl.BlockSpec(memory_space=pltpu.VMEM)],
        out_specs=pl.BlockSpec(memory_space=pltpu.VMEM),
        out_shape=jax.ShapeDtypeStruct((heads, BAND_ROWS * BAND_WINDOW), F32),
        compiler_params=pltpu.CompilerParams(vmem_limit_bytes=VMEM_LIMIT_BYTES),
        name="rel_bias",
    )(table)
    return out.reshape(heads, BAND_ROWS, BAND_WINDOW)


def _band_attn_kernel(q_ref, k_ref, v_ref, bias_ref, o_ref):
    g = pl.program_id(1)
    c, dh = CHUNK, B_HEAD_DIM
    n_slots = BAND_WINDOW // c
    first = g * BAND_GROUP - B_PREV_CHUNKS
    lane = lax.broadcasted_iota(jnp.int32, (BAND_ROWS, 2 * dh), 1)
    lane_kv = lax.broadcasted_iota(jnp.int32, (BAND_WINDOW, 2 * dh), 1)

    def attend(at_sequence_start):
        starts = [pl.multiple_of(jnp.maximum(first + j, 0) * c, c) for j in range(n_slots)]
        k_win = jnp.concatenate([k_ref[0, pl.ds(s, c), :] for s in starts], axis=0)
        v_win = jnp.concatenate([v_ref[0, pl.ds(s, c), :] for s in starts], axis=0)
        started = lax.broadcasted_iota(jnp.int32, (BAND_ROWS, BAND_WINDOW), 1) >= -first * c
        q_all = q_ref[...]
        for pair in range(B_HEADS // 2):
            cols = slice(pair * 2 * dh, (pair + 1) * 2 * dh)
            q_pair, k_pair, v_pair = q_all[:, cols], k_win[:, cols], v_win[:, cols]
            q_both = jnp.concatenate([jnp.where(lane < dh, q_pair, jnp.zeros_like(q_pair)),
                                      jnp.where(lane >= dh, q_pair, jnp.zeros_like(q_pair))],
                                     axis=0)
            s_both = lax.dot_general(q_both, k_pair, _NT, preferred_element_type=F32)
            outs = []
            for sub in range(2):
                s = s_both[sub * BAND_ROWS:(sub + 1) * BAND_ROWS] + bias_ref[2 * pair + sub]
                if at_sequence_start:
                    s = jnp.where(started, s, -jnp.inf)
                p = jnp.exp2(s - jnp.max(s, axis=-1, keepdims=True)).astype(BF16)
                ones_lane = dh if sub == 0 else 0
                own_kv = (lane_kv < dh) if sub == 0 else (lane_kv >= dh)
                v_h = jnp.where(own_kv, v_pair,
                                jnp.where(lane_kv == ones_lane, 1.0, 0.0).astype(v_pair.dtype))
                o = jnp.dot(p, v_h, preferred_element_type=F32)
                outs.append(o / o[:, ones_lane:ones_lane + 1])
            o_ref[:, cols] = jnp.where(lane < dh, outs[0], outs[1]).astype(o_ref.dtype)

    pl.when(first < 0)(functools.partial(attend, True))
    pl.when(first >= 0)(functools.partial(attend, False))


def band_attention(q, k, v, bias, batch, seq):
    rows, width = q.shape
    n_groups = seq // BAND_ROWS
    kv_spec = pl.BlockSpec((1, seq, width), lambda b, g: (b, 0, 0))
    return pl.pallas_call(
        _band_attn_kernel,
        grid=(batch, n_groups),
        in_specs=[pl.BlockSpec((BAND_ROWS, width), lambda b, g: (b * n_groups + g, 0)),
                  kv_spec, kv_spec, _resident(bias.shape)],
        out_specs=pl.BlockSpec((BAND_ROWS, width), lambda b, g: (b * n_groups + g, 0)),
        out_shape=jax.ShapeDtypeStruct((rows, width), BF16),
        compiler_params=_params("parallel", "arbitrary"),
        name="band_attn",
    )(q, k.reshape(batch, seq, width), v.reshape(batch, seq, width), bias)


def _fox_cum_kernel(blk, f_ref, bias_ref, tril_ref, o_ref):
    seq = f_ref.shape[1]
    carry = jnp.zeros((1, f_ref.shape[2]), F32)
    for c in range(seq // blk):
        x = f_ref[0, c * blk:(c + 1) * blk, :] + bias_ref[...]
        log_sig = jnp.minimum(x, 0.0) - jnp.log1p(jnp.exp(-jnp.abs(x)))
        cs = jnp.dot(tril_ref[...], log_sig, preferred_element_type=F32,
                     precision=lax.Precision.HIGHEST) + carry
        carry = cs[blk - 1:blk, :]
        o_ref[0, :, c * blk:(c + 1) * blk] = (cs * LOG2_E).T


def fox_cum(f, f_bias, batch, seq, blk=256):
    width = f.shape[-1]
    bias = jnp.pad(f_bias.astype(F32), (0, width - f_bias.shape[0])).reshape(1, width)
    tril = jnp.asarray(np.tril(np.ones((blk, blk), np.float32)))
    return pl.pallas_call(
        functools.partial(_fox_cum_kernel, blk),
        grid=(batch,),
        in_specs=[pl.BlockSpec((1, seq, width), lambda b: (b, 0, 0)), _resident((1, width)),
                  _resident((blk, blk))],
        out_specs=pl.BlockSpec((1, width, seq), lambda b: (b, 0, 0)),
        out_shape=jax.ShapeDtypeStruct((batch, width, seq), F32),
        compiler_params=_params("parallel"),
        name="fox_cum",
    )(f.reshape(batch, seq, width), bias, tril)


def _fox_attn_kernel(blk, q_ref, k_ref, v_ref, cum_ref, o_ref):
    step = pl.program_id(2)
    dh = C_HEAD_DIM
    n_blk = q_ref.shape[1] // blk
    lane = lax.broadcasted_iota(jnp.int32, (blk, 2 * dh), 1)
    causal = (lax.broadcasted_iota(jnp.int32, (blk, blk), 0)
              >= lax.broadcasted_iota(jnp.int32, (blk, blk), 1))

    def attend(n_past):
        past = n_past * blk
        lane_kv = lax.broadcasted_iota(jnp.int32, (past + blk, 2 * dh), 1)
        for pair in range(q_ref.shape[2] // (2 * dh)):
            cols = slice(pair * 2 * dh, (pair + 1) * 2 * dh)
            q_pair = q_ref[0, past:past + blk, cols]
            k_all, v_all = k_ref[0, :past + blk, cols], v_ref[0, :past + blk, cols]
            q_both = jnp.concatenate([jnp.where(lane < dh, q_pair, jnp.zeros_like(q_pair)),
                                      jnp.where(lane >= dh, q_pair, jnp.zeros_like(q_pair))],
                                     axis=0)
            s_both = lax.dot_general(q_both, k_all, _NT, preferred_element_type=F32)
            outs = []
            for sub in range(2):
                s = s_both[sub * blk:(sub + 1) * blk] - cum_ref[0, pair, sub:sub + 1, :past + blk]
                parts = [jnp.where(causal, s[:, past:], -jnp.inf)]
                if n_past:
                    parts.insert(0, s[:, :past])
                m = functools.reduce(jnp.maximum,
                                     [jnp.max(t, axis=-1, keepdims=True) for t in parts])
                p = jnp.concatenate([jnp.exp2(t - m).astype(BF16) for t in parts], axis=-1)
                ones_lane = dh if sub == 0 else 0
                own_kv = (lane_kv < dh) if sub == 0 else (lane_kv >= dh)
                v_h = jnp.where(own_kv, v_all,
                                jnp.where(lane_kv == ones_lane, 1.0, 0.0).astype(v_all.dtype))
                o = jnp.dot(p, v_h, preferred_element_type=F32)
                outs.append(o / o[:, ones_lane:ones_lane + 1])
            o_ref[0, past:past + blk, cols] = jnp.where(lane < dh, outs[0],
                                                        outs[1]).astype(o_ref.dtype)

    def attend_both(i):
        attend(i)
        attend(n_blk - 1 - i)

    for i in range(n_blk // 2):
        pl.when(step == i)(functools.partial(attend_both, i))


FOX_PAIRS_PER_STEP = 2


def fox_attention(q, k, v, cum_t, batch, seq, blk=FOX_BLOCK):
    width = q.shape[-1]
    pairs = C_HEADS // 2 // FOX_PAIRS_PER_STEP
    pair_w = 2 * C_HEAD_DIM * FOX_PAIRS_PER_STEP
    cum = cum_t[:, :C_HEADS, :].reshape(batch, C_HEADS // 2, 2, seq)
    q3, k3, v3 = (t.reshape(batch, seq, width) for t in (q, k, v))
    seq_spec = pl.BlockSpec((1, seq, pair_w), lambda b, p, i: (b, 0, p))
    assert (seq // blk) % 2 == 0
    out = pl.pallas_call(
        functools.partial(_fox_attn_kernel, blk),
        grid=(batch, pairs, seq // blk // 2),
        in_specs=[seq_spec, seq_spec, seq_spec,
                  pl.BlockSpec((1, FOX_PAIRS_PER_STEP, 2, seq), lambda b, p, i: (b, p, 0, 0))],
        out_specs=seq_spec,
        out_shape=jax.ShapeDtypeStruct((batch, seq, width), BF16),
        compiler_params=_params("parallel", "parallel", "arbitrary"),
        name="fox_attn",
    )(q3, k3, v3, cum)
    return out.reshape(batch * seq, width)


def _even_mixer(h, gain, w_in, lower_bound, out_gain, rel_table, w_out, batch, seq):
    a_w, b_w = A_HEADS * A_HEAD_DIM, B_HEADS * B_HEAD_DIM
    splits = [(a_w, BF16, 1.0), (a_w, F32, 1.0), (a_w, BF16, 1.0), (a_w, BF16, 1.0),
              (b_w, BF16, B_HEAD_DIM ** -0.5 * LOG2_E), (b_w, BF16, 1.0), (b_w, BF16, 1.0)]
    q_a, f_a, i_a, g_a, q_b, k_b, v_b = norm_proj(h, gain, w_in.astype(BF16), splits)
    o_a = hgrn2_mixer(q_a, f_a, i_a, g_a, lower_bound, out_gain, batch, seq)
    o_b = band_attention(q_b, k_b, v_b, rel_bias(rel_table), batch, seq)
    w_out = w_out.astype(BF16)
    return [o_a, o_b], [w_out[:a_w], w_out[a_w:]]


def _odd_mixer(h, gain, w_in, f_bias, w_out, batch, seq):
    c_w = C_HEADS * C_HEAD_DIM
    w_pad = jnp.pad(w_in, ((0, 0), (0, LANES - C_HEADS))).astype(BF16)
    splits = [(c_w, BF16, C_HEAD_DIM ** -0.5 * LOG2_E), (c_w, BF16, 1.0), (c_w, BF16, 1.0),
              (LANES, F32, 1.0)]
    q, k, v, f = norm_proj(h, gain, w_pad, splits)
    cum_t = fox_cum(f, f_bias, batch, seq)
    return [fox_attention(q, k, v, cum_t, batch, seq)], [w_out.astype(BF16)]


def kernel(x, mem, norm_mix, norm_xattn, norm_mem, norm_mlp, norm_final, w_in_ab, a_lb_logits,
           a_out_gain, b_rel_bias, w_out_ab, w_in_c, c_fgate_bias, w_out_c, w_xq, w_xkv, w_xo,
           w_up, w_down):
    batch, seq, d = x.shape
    mem_tokens = mem.shape[1]
    depth = norm_mix.shape[0]
    lb_w = jax.nn.softmax(a_lb_logits.astype(F32), axis=0)
    lower_bounds = jnp.cumsum(lb_w, axis=0) - lb_w[0]
    h = x.reshape(batch * seq, d)
    mem2 = mem.reshape(batch * mem_tokens, d)
    for layer in range(depth):
        if layer % 2 == 0:
            e = layer // 2
            acts, w_outs = _even_mixer(h, norm_mix[layer], w_in_ab[e], lower_bounds[e],
                                       a_out_gain[e], b_rel_bias[e], w_out_ab[e], batch, seq)
        else:
            o = layer // 2
            acts, w_outs = _odd_mixer(h, norm_mix[layer], w_in_c[o], c_fgate_bias[o], w_out_c[o],
                                      batch, seq)
        k_mem, v_mem = norm_proj(mem2, norm_mem[layer], w_xkv[layer].astype(BF16),
                                 [(d, BF16, 1.0), (d, BF16, 1.0)])
        h = layer_tail(h, acts, w_outs, norm_xattn[layer], w_xq[layer].astype(BF16),
                       k_mem.reshape(batch, mem_tokens, d), v_mem.reshape(batch, mem_tokens, d),
                       w_xo[layer].astype(BF16), norm_mlp[layer], w_up[layer].astype(BF16),
                       w_down[layer].astype(BF16), seq,
                       g_final=norm_final if layer == depth - 1 else None)
    return h.reshape(batch, seq, d)
```

```python
import functools

import numpy as np
import jax
import jax.numpy as jnp
from jax import lax
from jax.experimental import pallas as pl
from jax.experimental.pallas import tpu as pltpu

F32 = jnp.float32
BF16 = jnp.bfloat16

EPS = 1e-6
LOG2_E = 1.4426950408889634
CHUNK = 64
A_HEADS = 4
A_HEAD_DIM = 128
B_HEADS = 8
B_HEAD_DIM = 64
B_PREV_CHUNKS = 8
B_BAND = B_PREV_CHUNKS + 1
REL_CLIP = 2 * CHUNK
C_HEADS = 16
C_HEAD_DIM = 64
X_HEADS = 4
HGRN_LEVELS = 6

LANES = 128
VMEM_LIMIT_BYTES = 48 * 1024 * 1024
TAIL_VMEM_LIMIT_BYTES = 56 * 1024 * 1024

ROW_TILE = 512
FOX_BLOCK = 512

_NT = (((1,), (1,)), ((), ()))
_TN = (((0,), (0,)), ((), ()))


def _params(*semantics):
    return pltpu.CompilerParams(dimension_semantics=semantics, vmem_limit_bytes=VMEM_LIMIT_BYTES)


def _rms_norm(x, gain):
    return x * lax.rsqrt(jnp.mean(x * x, axis=-1, keepdims=True) + EPS) * gain


def _resident(shape):
    return pl.BlockSpec(shape, lambda *_: (0,) * len(shape), pipeline_mode=pl.Buffered(1))


def _resident_layer(stack_shape, layer):
    return pl.BlockSpec((None,) + tuple(stack_shape[1:]),
                        lambda *_: (layer,) + (0,) * (len(stack_shape) - 1),
                        pipeline_mode=pl.Buffered(1))


def _norm_proj_kernel(scales, x_ref, g_ref, w_ref, *out_refs):
    xn = _rms_norm(x_ref[...], g_ref[...]).astype(BF16)
    off = 0
    for o_ref, scale in zip(out_refs, scales):
        width = o_ref.shape[-1]
        y = jnp.dot(xn, w_ref[:, off:off + width], preferred_element_type=F32)
        o_ref[...] = (y if scale == 1.0 else y * scale).astype(o_ref.dtype)
        off += width


def norm_proj(x, gain, w_stack, layer, splits, row_tile=ROW_TILE):
    rows, d = x.shape
    n = w_stack.shape[2]
    assert sum(width for width, _, _ in splits) == n and rows % row_tile == 0
    return pl.pallas_call(
        functools.partial(_norm_proj_kernel, tuple(scale for _, _, scale in splits)),
        grid=(rows // row_tile,),
        in_specs=[pl.BlockSpec((row_tile, d), lambda i: (i, 0)),
                  _resident((1, d)), _resident_layer(w_stack.shape, layer)],
        out_specs=[pl.BlockSpec((row_tile, width), lambda i: (i, 0)) for width, _, _ in splits],
        out_shape=[jax.ShapeDtypeStruct((rows, width), dt) for width, dt, _ in splits],
        compiler_params=_params("parallel"),
        name="norm_proj",
    )(x, gain.reshape(1, d), w_stack)


def _xattn_update(h, gain, wq_ref, k_ref, v_ref, wo_ref):
    dh = h.shape[-1] // X_HEADS
    xn = _rms_norm(h, gain).astype(BF16)
    q = (jnp.dot(xn, wq_ref[...], preferred_element_type=F32) * (dh ** -0.5 * LOG2_E)).astype(BF16)
    outs = []
    for hd in range(X_HEADS):
        cols = slice(hd * dh, (hd + 1) * dh)
        s = lax.dot_general(q[:, cols], k_ref[0, :, cols], _NT, preferred_element_type=F32)
        p = jnp.exp2(s - jnp.max(s, axis=-1, keepdims=True))
        denom = jnp.sum(p, axis=-1, keepdims=True)
        o = jnp.dot(p.astype(BF16), v_ref[0, :, cols], preferred_element_type=F32)
        outs.append((o / denom).astype(BF16))
    return h + jnp.dot(jnp.concatenate(outs, axis=-1), wo_ref[...], preferred_element_type=F32)


def _mlp_update(h, gain, wu_ref, wd_ref, ff_chunk):
    xn = _rms_norm(h, gain).astype(BF16)
    acc = h
    for c in range(wu_ref.shape[1] // ff_chunk):
        cols = slice(c * ff_chunk, (c + 1) * ff_chunk)
        u = jnp.maximum(jnp.dot(xn, wu_ref[:, cols], preferred_element_type=F32), 0.0)
        acc = acc + jnp.dot((u * u).astype(BF16), wd_ref[cols, :], preferred_element_type=F32)
    return acc


def _layer_tail_kernel(n_act, final, ff_chunk, h_ref, *refs):
    a_refs = refs[:n_act]
    (wout_ref, gx_ref, wq_ref, k_ref, v_ref, wo_ref, gm_ref, wu_ref,
     wd_ref) = refs[n_act:n_act + 9]
    o_ref = refs[-1]
    h = h_ref[...]
    off = 0
    for a_ref in a_refs:
        width = a_ref.shape[1]
        h = h + jnp.dot(a_ref[...], wout_ref[off:off + width, :], preferred_element_type=F32)
        off += width
    h = _xattn_update(h, gx_ref[...], wq_ref, k_ref, v_ref, wo_ref)
    h = _mlp_update(h, gm_ref[...], wu_ref, wd_ref, ff_chunk)
    if final:
        h = _rms_norm(h, refs[-2][...])
    o_ref[...] = h


def layer_tail(h, acts, w_out, mixer, g_xattn, w_q, k_mem, v_mem, w_o, g_mlp, w_up, w_down, layer,
               seq, g_final=None, row_tile=ROW_TILE, ff_chunk=1024):
    rows, d = h.shape
    tiles_per_seq = seq // row_tile
    row_spec = lambda width: pl.BlockSpec((row_tile, width), lambda i: (i, 0))
    mem_spec = pl.BlockSpec((1,) + k_mem.shape[1:], lambda i: (i // tiles_per_seq, 0, 0))
    gains = [g.reshape(1, d) for g in (g_xattn, g_mlp)]
    operands = [h, *acts, w_out, gains[0], w_q, k_mem, v_mem, w_o, gains[1], w_up, w_down]
    in_specs = ([row_spec(d)] + [row_spec(a.shape[1]) for a in acts]
                + [_resident_layer(w_out.shape, mixer), _resident((1, d)),
                   _resident_layer(w_q.shape, layer), mem_spec, mem_spec,
                   _resident_layer(w_o.shape, layer), _resident((1, d)),
                   _resident_layer(w_up.shape, layer), _resident_layer(w_down.shape, layer)])
    if g_final is not None:
        operands.append(g_final.reshape(1, d))
        in_specs.append(_resident((1, d)))
    return pl.pallas_call(
        functools.partial(_layer_tail_kernel, len(acts), g_final is not None, ff_chunk),
        grid=(rows // row_tile,),
        in_specs=in_specs,
        out_specs=row_spec(d),
        out_shape=jax.ShapeDtypeStruct((rows, d), F32),
        compiler_params=pltpu.CompilerParams(dimension_semantics=("parallel",),
                                             vmem_limit_bytes=TAIL_VMEM_LIMIT_BYTES),
        name="layer_tail",
    )(*operands)


def _hgrn_tables():
    c = CHUNK
    t = np.arange(c)[:, None]
    j = np.arange(c)[None, :]
    blocks = [(j <= t), (j > t)]
    masks = [np.eye(c, dtype=bool)]
    for level in range(HGRN_LEVELS):
        size = c >> level
        half = size // 2
        ref = (t // size) * size + half - 1
        upper = (t % size) >= half
        blocks.append(np.where(upper, (j > ref) & (j <= t), (j > t) & (j <= ref)))
        tt, ss = np.arange(c)[:, None], np.arange(c)[None, :]
        masks.append((tt // size == ss // size) & ((tt % size) >= half) & ((ss % size) < half))
    w = np.concatenate(blocks, axis=0).astype(np.float32)
    return np.concatenate([w, w], axis=1), np.stack(masks).astype(np.float32)


def _hgrn_kernel(q_ref, f_ref, i_ref, g_ref, lb_ref, gain_ref, ww_ref, mask_ref, o_ref, st_ref):
    c, dk = CHUNK, A_HEAD_DIM

    @pl.when(pl.program_id(1) == 0)
    def _():
        st_ref[...] = jnp.zeros_like(st_ref)

    z = f_ref[...]
    lb = lb_ref[...]
    log_f = jnp.log2(lb + (1.0 - lb) * jax.nn.sigmoid(z))
    kk = (1.0 - lb) * jax.nn.sigmoid(-z)
    qf = q_ref[...].astype(F32)
    qs = qf * jax.nn.sigmoid(qf) * dk ** -0.5
    hi = log_f.astype(BF16)
    lo = (log_f - hi.astype(F32)).astype(BF16)
    row = lax.broadcasted_iota(jnp.int32, (c, q_ref.shape[1]), 0)

    head_cols = [slice(hd * dk, (hd + 1) * dk) for hd in range(A_HEADS)]
    n_chunks = q_ref.shape[0] // c

    independent = []
    for ci in range(n_chunks):
        rows = slice(ci * c, (ci + 1) * c)
        x = jnp.dot(ww_ref[...], jnp.concatenate([hi[rows], lo[rows]], axis=0),
                    preferred_element_type=F32)
        b = x[0:c]
        chunk_decay = jnp.exp2(b[c - 1:c, :])
        q_c, k_c, v = qs[rows], kk[rows], i_ref[rows, :]
        q_in = (q_c * jnp.exp2(b)).astype(BF16)
        k_out = (k_c * jnp.exp2(x[c:2 * c])).astype(BF16)
        sides = [(q_c.astype(BF16), k_c.astype(BF16))]
        for level in range(HGRN_LEVELS):
            half = (c >> level) // 2
            e = jnp.exp2(x[(2 + level) * c:(3 + level) * c])
            m = (jnp.where((row & half) != 0, q_c, k_c) * e).astype(BF16)
            sides.append((m, m))
        o_intra, upd_t = [], []
        for cols in head_cols:
            a = sum(lax.dot_general(lhs[:, cols], rhs[:, cols], _NT,
                                    preferred_element_type=F32) * mask_ref[n]
                    for n, (lhs, rhs) in enumerate(sides))
            o_intra.append(jnp.dot(a.astype(BF16), v[:, cols], preferred_element_type=F32))
            upd_t.append(lax.dot_general(v[:, cols], k_out[:, cols], _TN,
                                         preferred_element_type=F32))
        independent.append((q_in, chunk_decay, o_intra, upd_t))

    states = [st_ref[hd] for hd in range(A_HEADS)]
    for ci, (q_in, chunk_decay, o_intra, upd_t) in enumerate(independent):
        outs = []
        for hd, cols in enumerate(head_cols):
            o = o_intra[hd] + lax.dot_general(q_in[:, cols], states[hd].astype(BF16), _NT,
                                              preferred_element_type=F32)
            states[hd] = states[hd] * chunk_decay[:, cols] + upd_t[hd]
            outs.append(_rms_norm(o, gain_ref[:, cols]))
        rows = slice(ci * c, (ci + 1) * c)
        gf = g_ref[rows, :].astype(F32)
        o_ref[rows, :] = (jnp.concatenate(outs, axis=-1)
                          * (gf * jax.nn.sigmoid(gf))).astype(o_ref.dtype)
    for hd in range(A_HEADS):
        st_ref[hd] = states[hd]


HGRN_GROUP = 4


def hgrn2_mixer(q, f, i, g, lower_bound, out_gain, batch, seq):
    rows, width = q.shape
    n_chunks = seq // (HGRN_GROUP * CHUNK)
    ww, masks = _hgrn_tables()
    act = pl.BlockSpec((HGRN_GROUP * CHUNK, width), lambda b, n: (b * n_chunks + n, 0))
    return pl.pallas_call(
        _hgrn_kernel,
        grid=(batch, n_chunks),
        in_specs=[act, act, act, act, _resident((1, width)), _resident((1, width)),
                  _resident(ww.shape), _resident(masks.shape)],
        out_specs=act,
        out_shape=jax.ShapeDtypeStruct((rows, width), BF16),
        scratch_shapes=[pltpu.VMEM((A_HEADS, A_HEAD_DIM, A_HEAD_DIM), F32)],
        compiler_params=_params("parallel", "arbitrary"),
        name="hgrn2",
    )(q, f, i, g, lower_bound.reshape(1, width), out_gain.reshape(1, width),
      jnp.asarray(ww, BF16), jnp.asarray(masks))


BAND_GROUP = 4
BAND_ROWS = BAND_GROUP * CHUNK
BAND_WINDOW = (BAND_GROUP + B_PREV_CHUNKS) * CHUNK
BIAS_SPAN = BAND_ROWS + BAND_WINDOW


def _rel_bias_kernel(table_ref, o_ref):
    heads, n_idx = table_ref.shape
    e = lax.broadcasted_iota(jnp.int32, (n_idx, BIAS_SPAN), 1)
    distance = B_PREV_CHUNKS * CHUNK + BAND_ROWS - 1 - e
    idx = jnp.clip(distance, -REL_CLIP, REL_CLIP) + REL_CLIP
    onehot = (lax.broadcasted_iota(jnp.int32, (n_idx, BIAS_SPAN), 0) == idx).astype(F32)
    by_offset = jnp.dot(table_ref[...], onehot, preferred_element_type=F32,
                        precision=lax.Precision.HIGHEST) * LOG2_E
    col = lax.broadcasted_iota(jnp.int32, (heads, BAND_WINDOW), 1)
    for r in range(BAND_ROWS):
        band_start = (r // CHUNK) * CHUNK
        in_band = (col >= band_start) & (col < band_start + B_BAND * CHUNK)
        shift = BAND_ROWS - 1 - r
        o_ref[:, r * BAND_WINDOW:(r + 1) * BAND_WINDOW] = jnp.where(
            in_band, by_offset[:, shift:shift + BAND_WINDOW], -jnp.inf)


def rel_bias(rel_table):
    heads, n_rel = rel_table.shape
    n_idx = -(-n_rel // LANES) * LANES
    table = jnp.pad(rel_table.astype(F32), ((0, 0), (0, n_idx - n_rel)))
    out = pl.pallas_call(
        _rel_bias_kernel,
        in_specs=[pl.BlockSpec(memory_space=pltpu.VMEM)],
        out_specs=pl.BlockSpec(memory_space=pltpu.VMEM),
        out_shape=jax.ShapeDtypeStruct((heads, BAND_ROWS * BAND_WINDOW), F32),
        compiler_params=pltpu.CompilerParams(vmem_limit_bytes=VMEM_LIMIT_BYTES),
        name="rel_bias",
    )(table)
    return out.reshape(heads, BAND_ROWS, BAND_WINDOW)


def _band_attn_kernel(q_ref, k_ref, v_ref, bias_ref, o_ref):
    g = pl.program_id(1)
    c, dh = CHUNK, B_HEAD_DIM
    n_slots = BAND_WINDOW // c
    first = g * BAND_GROUP - B_PREV_CHUNKS
    lane = lax.broadcasted_iota(jnp.int32, (BAND_ROWS, 2 * dh), 1)
    lane_kv = lax.broadcasted_iota(jnp.int32, (BAND_WINDOW, 2 * dh), 1)

    def attend(at_sequence_start):
        starts = [pl.multiple_of(jnp.maximum(first + j, 0) * c, c) for j in range(n_slots)]
        k_win = jnp.concatenate([k_ref[0, pl.ds(s, c), :] for s in starts], axis=0)
        v_win = jnp.concatenate([v_ref[0, pl.ds(s, c), :] for s in starts], axis=0)
        started = lax.broadcasted_iota(jnp.int32, (BAND_ROWS, BAND_WINDOW), 1) >= -first * c
        q_all = q_ref[...]
        for pair in range(B_HEADS // 2):
            cols = slice(pair * 2 * dh, (pair + 1) * 2 * dh)
            q_pair, k_pair, v_pair = q_all[:, cols], k_win[:, cols], v_win[:, cols]
            q_both = jnp.concatenate([jnp.where(lane < dh, q_pair, jnp.zeros_like(q_pair)),
                                      jnp.where(lane >= dh, q_pair, jnp.zeros_like(q_pair))],
                                     axis=0)
            s_both = lax.dot_general(q_both, k_pair, _NT, preferred_element_type=F32)
            outs = []
            for sub in range(2):
                s = s_both[sub * BAND_ROWS:(sub + 1) * BAND_ROWS] + bias_ref[2 * pair + sub]
                if at_sequence_start:
                    s = jnp.where(started, s, -jnp.inf)
                p = jnp.exp2(s - jnp.max(s, axis=-1, keepdims=True)).astype(BF16)
                ones_lane = dh if sub == 0 else 0
                own_kv = (lane_kv < dh) if sub == 0 else (lane_kv >= dh)
                v_h = jnp.where(own_kv, v_pair,
                                jnp.where(lane_kv == ones_lane, 1.0, 0.0).astype(v_pair.dtype))
                o = jnp.dot(p, v_h, preferred_element_type=F32)
                outs.append(o / o[:, ones_lane:ones_lane + 1])
            o_ref[:, cols] = jnp.where(lane < dh, outs[0], outs[1]).astype(o_ref.dtype)

    pl.when(first < 0)(functools.partial(attend, True))
    pl.when(first >= 0)(functools.partial(attend, False))


def band_attention(q, k, v, bias, batch, seq):
    rows, width = q.shape
    n_groups = seq // BAND_ROWS
    kv_spec = pl.BlockSpec((1, seq, width), lambda b, g: (b, 0, 0))
    return pl.pallas_call(
        _band_attn_kernel,
        grid=(batch, n_groups),
        in_specs=[pl.BlockSpec((BAND_ROWS, width), lambda b, g: (b * n_groups + g, 0)),
                  kv_spec, kv_spec, _resident(bias.shape)],
        out_specs=pl.BlockSpec((BAND_ROWS, width), lambda b, g: (b * n_groups + g, 0)),
        out_shape=jax.ShapeDtypeStruct((rows, width), BF16),
        compiler_params=_params("parallel", "arbitrary"),
        name="band_attn",
    )(q, k.reshape(batch, seq, width), v.reshape(batch, seq, width), bias)


def _fox_cum_kernel(blk, f_ref, bias_ref, tril_ref, o_ref):
    seq = f_ref.shape[1]
    carry = jnp.zeros((1, f_ref.shape[2]), F32)
    for c in range(seq // blk):
        x = f_ref[0, c * blk:(c + 1) * blk, :] + bias_ref[...]
        log_sig = jnp.minimum(x, 0.0) - jnp.log1p(jnp.exp(-jnp.abs(x)))
        cs = jnp.dot(tril_ref[...], log_sig, preferred_element_type=F32,
                     precision=lax.Precision.HIGHEST) + carry
        carry = cs[blk - 1:blk, :]
        o_ref[0, :, c * blk:(c + 1) * blk] = (cs * LOG2_E).T[:o_ref.shape[1]]


def fox_cum(f, f_bias, batch, seq, blk=256):
    width = f.shape[-1]
    n_gates = f_bias.shape[0]
    bias = jnp.pad(f_bias.astype(F32), (0, width - n_gates)).reshape(1, width)
    tril = jnp.asarray(np.tril(np.ones((blk, blk), np.float32)))
    return pl.pallas_call(
        functools.partial(_fox_cum_kernel, blk),
        grid=(batch,),
        in_specs=[pl.BlockSpec((1, seq, width), lambda b: (b, 0, 0)), _resident((1, width)),
                  _resident((blk, blk))],
        out_specs=pl.BlockSpec((1, n_gates, seq), lambda b: (b, 0, 0)),
        out_shape=jax.ShapeDtypeStruct((batch, n_gates, seq), F32),
        compiler_params=_params("parallel"),
        name="fox_cum",
    )(f.reshape(batch, seq, width), bias, tril)


def _fox_attn_kernel(blk, q_ref, k_ref, v_ref, cum_ref, o_ref):
    step = pl.program_id(2)
    dh = C_HEAD_DIM
    n_blk = q_ref.shape[1] // blk
    lane = lax.broadcasted_iota(jnp.int32, (blk, 2 * dh), 1)
    causal = (lax.broadcasted_iota(jnp.int32, (blk, blk), 0)
              >= lax.broadcasted_iota(jnp.int32, (blk, blk), 1))

    def attend(n_past):
        past = n_past * blk
        lane_kv = lax.broadcasted_iota(jnp.int32, (past + blk, 2 * dh), 1)
        for pair in range(q_ref.shape[2] // (2 * dh)):
            cols = slice(pair * 2 * dh, (pair + 1) * 2 * dh)
            q_pair = q_ref[0, past:past + blk, cols]
            k_all, v_all = k_ref[0, :past + blk, cols], v_ref[0, :past + blk, cols]
            q_both = jnp.concatenate([jnp.where(lane < dh, q_pair, jnp.zeros_like(q_pair)),
                                      jnp.where(lane >= dh, q_pair, jnp.zeros_like(q_pair))],
                                     axis=0)
            s_both = lax.dot_general(q_both, k_all, _NT, preferred_element_type=F32)
            outs = []
            for sub in range(2):
                s = s_both[sub * blk:(sub + 1) * blk] - cum_ref[0, pair, sub:sub + 1, :past + blk]
                parts = [jnp.where(causal, s[:, past:], -jnp.inf)]
                if n_past:
                    parts.insert(0, s[:, :past])
                m = functools.reduce(jnp.maximum,
                                     [jnp.max(t, axis=-1, keepdims=True) for t in parts])
                p = jnp.concatenate([jnp.exp2(t - m).astype(BF16) for t in parts], axis=-1)
                ones_lane = dh if sub == 0 else 0
                own_kv = (lane_kv < dh) if sub == 0 else (lane_kv >= dh)
                v_h = jnp.where(own_kv, v_all,
                                jnp.where(lane_kv == ones_lane, 1.0, 0.0).astype(v_all.dtype))
                o = jnp.dot(p, v_h, preferred_element_type=F32)
                outs.append(o / o[:, ones_lane:ones_lane + 1])
            o_ref[0, past:past + blk, cols] = jnp.where(lane < dh, outs[0],
                                                        outs[1]).astype(o_ref.dtype)

    def attend_both(i):
        attend(i)
        attend(n_blk - 1 - i)

    for i in range(n_blk // 2):
        pl.when(step == i)(functools.partial(attend_both, i))


FOX_PAIRS_PER_STEP = 2


def fox_attention(q, k, v, cum_t, batch, seq, blk=FOX_BLOCK):
    width = q.shape[-1]
    pairs = C_HEADS // 2 // FOX_PAIRS_PER_STEP
    pair_w = 2 * C_HEAD_DIM * FOX_PAIRS_PER_STEP
    cum = cum_t.reshape(batch, C_HEADS // 2, 2, seq)
    q3, k3, v3 = (t.reshape(batch, seq, width) for t in (q, k, v))
    seq_spec = pl.BlockSpec((1, seq, pair_w), lambda b, p, i: (b, 0, p))
    assert (seq // blk) % 2 == 0
    out = pl.pallas_call(
        functools.partial(_fox_attn_kernel, blk),
        grid=(batch, pairs, seq // blk // 2),
        in_specs=[seq_spec, seq_spec, seq_spec,
                  pl.BlockSpec((1, FOX_PAIRS_PER_STEP, 2, seq), lambda b, p, i: (b, p, 0, 0))],
        out_specs=seq_spec,
        out_shape=jax.ShapeDtypeStruct((batch, seq, width), BF16),
        compiler_params=_params("parallel", "parallel", "arbitrary"),
        name="fox_attn",
    )(q3, k3, v3, cum)
    return out.reshape(batch * seq, width)


def _even_mixer(h, gain, w_in_stack, e, lower_bound, out_gain, rel_table, batch, seq):
    a_w, b_w = A_HEADS * A_HEAD_DIM, B_HEADS * B_HEAD_DIM
    splits = [(a_w, BF16, 1.0), (a_w, F32, 1.0), (a_w, BF16, 1.0), (a_w, BF16, 1.0),
              (b_w, BF16, B_HEAD_DIM ** -0.5 * LOG2_E), (b_w, BF16, 1.0), (b_w, BF16, 1.0)]
    q_a, f_a, i_a, g_a, q_b, k_b, v_b = norm_proj(h, gain, w_in_stack, e, splits)
    o_a = hgrn2_mixer(q_a, f_a, i_a, g_a, lower_bound, out_gain, batch, seq)
    o_b = band_attention(q_b, k_b, v_b, rel_bias(rel_table), batch, seq)
    return [o_a, o_b]


def _pad_gate_columns(w_in_c):
    return jnp.pad(w_in_c, ((0, 0), (0, 0), (0, LANES - C_HEADS)))


def _odd_mixer(h, gain, w_in_stack, o, f_bias, batch, seq):
    c_w = C_HEADS * C_HEAD_DIM
    splits = [(c_w, BF16, C_HEAD_DIM ** -0.5 * LOG2_E), (c_w, BF16, 1.0), (c_w, BF16, 1.0),
              (LANES, F32, 1.0)]
    q, k, v, f = norm_proj(h, gain, w_in_stack, o, splits)
    cum_t = fox_cum(f, f_bias, batch, seq)
    return [fox_attention(q, k, v, cum_t, batch, seq)]


def kernel(x, mem, norm_mix, norm_xattn, norm_mem, norm_mlp, norm_final, w_in_ab, a_lb_logits,
           a_out_gain, b_rel_bias, w_out_ab, w_in_c, c_fgate_bias, w_out_c, w_xq, w_xkv, w_xo,
           w_up, w_down):
    batch, seq, d = x.shape
    mem_tokens = mem.shape[1]
    depth = norm_mix.shape[0]
    lb_w = jax.nn.softmax(a_lb_logits.astype(F32), axis=0)
    lower_bounds = jnp.cumsum(lb_w, axis=0) - lb_w[0]
    h = x.reshape(batch * seq, d)
    mem2 = mem.reshape(batch * mem_tokens, d)
    w_in_ab, w_in_c = w_in_ab.astype(BF16), _pad_gate_columns(w_in_c).astype(BF16)
    w_out_ab, w_out_c = w_out_ab.astype(BF16), w_out_c.astype(BF16)
    w_xq, w_xkv, w_xo = w_xq.astype(BF16), w_xkv.astype(BF16), w_xo.astype(BF16)
    w_up, w_down = w_up.astype(BF16), w_down.astype(BF16)
    for layer in range(depth):
        mixer = layer // 2
        if layer % 2 == 0:
            acts = _even_mixer(h, norm_mix[layer], w_in_ab, mixer, lower_bounds[mixer],
                               a_out_gain[mixer], b_rel_bias[mixer], batch, seq)
            w_out = w_out_ab
        else:
            acts = _odd_mixer(h, norm_mix[layer], w_in_c, mixer, c_fgate_bias[mixer], batch, seq)
            w_out = w_out_c
        k_mem, v_mem = norm_proj(mem2, norm_mem[layer], w_xkv, layer,
                                 [(d, BF16, 1.0), (d, BF16, 1.0)])
        h = layer_tail(h, acts, w_out, mixer, norm_xattn[layer], w_xq,
                       k_mem.reshape(batch, mem_tokens, d), v_mem.reshape(batch, mem_tokens, d),
                       w_xo, norm_mlp[layer], w_up, w_down, layer, seq,
                       g_final=norm_final if layer == depth - 1 else None)
    return h.reshape(batch, seq, d)
```

```python
import functools

import numpy as np
import jax
import jax.numpy as jnp
from jax import lax
from jax.experimental import pallas as pl
from jax.experimental.pallas import tpu as pltpu

F32 = jnp.float32
BF16 = jnp.bfloat16

EPS = 1e-6
LOG2_E = 1.4426950408889634
CHUNK = 64
A_HEADS = 4
A_HEAD_DIM = 128
B_HEADS = 8
B_HEAD_DIM = 64
B_PREV_CHUNKS = 8
B_BAND = B_PREV_CHUNKS + 1
REL_CLIP = 2 * CHUNK
C_HEADS = 16
C_HEAD_DIM = 64
X_HEADS = 4
HGRN_LEVELS = 6

LANES = 128
VMEM_LIMIT_BYTES = 48 * 1024 * 1024
TAIL_VMEM_LIMIT_BYTES = 56 * 1024 * 1024

ROW_TILE = 512
FOX_BLOCK = 512

_NT = (((1,), (1,)), ((), ()))
_TN = (((0,), (0,)), ((), ()))


def _params(*semantics):
    return pltpu.CompilerParams(dimension_semantics=semantics, vmem_limit_bytes=VMEM_LIMIT_BYTES)


def _rms_norm(x, gain):
    return x * lax.rsqrt(jnp.mean(x * x, axis=-1, keepdims=True) + EPS) * gain


def _resident(shape):
    return pl.BlockSpec(shape, lambda *_: (0,) * len(shape), pipeline_mode=pl.Buffered(1))


def _resident_layer(stack_shape, layer):
    return pl.BlockSpec((None,) + tuple(stack_shape[1:]),
                        lambda *_: (layer,) + (0,) * (len(stack_shape) - 1),
                        pipeline_mode=pl.Buffered(1))


def _norm_proj_kernel(scales, x_ref, g_ref, w_ref, *out_refs):
    xn = _rms_norm(x_ref[...], g_ref[...]).astype(BF16)
    off = 0
    for o_ref, scale in zip(out_refs, scales):
        width = o_ref.shape[-1]
        y = jnp.dot(xn, w_ref[:, off:off + width], preferred_element_type=F32)
        o_ref[...] = (y if scale == 1.0 else y * scale).astype(o_ref.dtype)
        off += width


def norm_proj(x, gain, w_stack, layer, splits, row_tile=ROW_TILE):
    rows, d = x.shape
    n = w_stack.shape[2]
    assert sum(width for width, _, _ in splits) == n and rows % row_tile == 0
    return pl.pallas_call(
        functools.partial(_norm_proj_kernel, tuple(scale for _, _, scale in splits)),
        grid=(rows // row_tile,),
        in_specs=[pl.BlockSpec((row_tile, d), lambda i: (i, 0)),
                  _resident((1, d)), _resident_layer(w_stack.shape, layer)],
        out_specs=[pl.BlockSpec((row_tile, width), lambda i: (i, 0)) for width, _, _ in splits],
        out_shape=[jax.ShapeDtypeStruct((rows, width), dt) for width, dt, _ in splits],
        compiler_params=_params("parallel"),
        name="norm_proj",
    )(x, gain.reshape(1, d), w_stack)


def _xattn_update(h, gain, wq_ref, k_ref, v_ref, wo_ref):
    dh = h.shape[-1] // X_HEADS
    xn = _rms_norm(h, gain).astype(BF16)
    q = (jnp.dot(xn, wq_ref[...], preferred_element_type=F32) * (dh ** -0.5 * LOG2_E)).astype(BF16)
    outs = []
    for hd in range(X_HEADS):
        cols = slice(hd * dh, (hd + 1) * dh)
        s = lax.dot_general(q[:, cols], k_ref[0, :, cols], _NT, preferred_element_type=F32)
        p = jnp.exp2(s - jnp.max(s, axis=-1, keepdims=True))
        denom = jnp.sum(p, axis=-1, keepdims=True)
        o = jnp.dot(p.astype(BF16), v_ref[0, :, cols], preferred_element_type=F32)
        outs.append((o / denom).astype(BF16))
    return h + jnp.dot(jnp.concatenate(outs, axis=-1), wo_ref[...], preferred_element_type=F32)


def _mlp_update(h, gain, wu_ref, wd_ref, ff_chunk):
    xn = _rms_norm(h, gain).astype(BF16)
    acc = h
    for c in range(wu_ref.shape[1] // ff_chunk):
        cols = slice(c * ff_chunk, (c + 1) * ff_chunk)
        u = jnp.maximum(jnp.dot(xn, wu_ref[:, cols], preferred_element_type=F32), 0.0)
        acc = acc + jnp.dot((u * u).astype(BF16), wd_ref[cols, :], preferred_element_type=F32)
    return acc


def _layer_tail_kernel(n_act, final, ff_chunk, h_ref, *refs):
    a_refs = refs[:n_act]
    (wout_ref, gx_ref, wq_ref, k_ref, v_ref, wo_ref, gm_ref, wu_ref,
     wd_ref) = refs[n_act:n_act + 9]
    o_ref = refs[-1]
    h = h_ref[...]
    off = 0
    for a_ref in a_refs:
        width = a_ref.shape[1]
        h = h + jnp.dot(a_ref[...], wout_ref[off:off + width, :], preferred_element_type=F32)
        off += width
    h = _xattn_update(h, gx_ref[...], wq_ref, k_ref, v_ref, wo_ref)
    h = _mlp_update(h, gm_ref[...], wu_ref, wd_ref, ff_chunk)
    if final:
        h = _rms_norm(h, refs[-2][...])
    o_ref[...] = h


def layer_tail(h, acts, w_out, mixer, g_xattn, w_q, k_mem, v_mem, w_o, g_mlp, w_up, w_down, layer,
               seq, g_final=None, row_tile=ROW_TILE, ff_chunk=1024):
    rows, d = h.shape
    tiles_per_seq = seq // row_tile
    row_spec = lambda width: pl.BlockSpec((row_tile, width), lambda i: (i, 0))
    mem_spec = pl.BlockSpec((1,) + k_mem.shape[1:], lambda i: (i // tiles_per_seq, 0, 0))
    gains = [g.reshape(1, d) for g in (g_xattn, g_mlp)]
    operands = [h, *acts, w_out, gains[0], w_q, k_mem, v_mem, w_o, gains[1], w_up, w_down]
    in_specs = ([row_spec(d)] + [row_spec(a.shape[1]) for a in acts]
                + [_resident_layer(w_out.shape, mixer), _resident((1, d)),
                   _resident_layer(w_q.shape, layer), mem_spec, mem_spec,
                   _resident_layer(w_o.shape, layer), _resident((1, d)),
                   _resident_layer(w_up.shape, layer), _resident_layer(w_down.shape, layer)])
    if g_final is not None:
        operands.append(g_final.reshape(1, d))
        in_specs.append(_resident((1, d)))
    return pl.pallas_call(
        functools.partial(_layer_tail_kernel, len(acts), g_final is not None, ff_chunk),
        grid=(rows // row_tile,),
        in_specs=in_specs,
        out_specs=row_spec(d),
        out_shape=jax.ShapeDtypeStruct((rows, d), F32),
        compiler_params=pltpu.CompilerParams(dimension_semantics=("parallel",),
                                             vmem_limit_bytes=TAIL_VMEM_LIMIT_BYTES),
        name="layer_tail",
    )(*operands)


def _hgrn_tables():
    c = CHUNK
    t = np.arange(c)[:, None]
    j = np.arange(c)[None, :]
    blocks = [(j <= t), (j > t)]
    masks = [np.eye(c, dtype=bool)]
    for level in range(HGRN_LEVELS):
        size = c >> level
        half = size // 2
        ref = (t // size) * size + half - 1
        upper = (t % size) >= half
        blocks.append(np.where(upper, (j > ref) & (j <= t), (j > t) & (j <= ref)))
        tt, ss = np.arange(c)[:, None], np.arange(c)[None, :]
        masks.append((tt // size == ss // size) & ((tt % size) >= half) & ((ss % size) < half))
    w = np.concatenate(blocks, axis=0).astype(np.float32)
    return np.concatenate([w, w], axis=1), np.stack(masks).astype(np.float32)


def _hgrn_kernel(q_ref, f_ref, i_ref, g_ref, lb_ref, gain_ref, ww_ref, mask_ref, o_ref, st_ref):
    c, dk = CHUNK, A_HEAD_DIM

    @pl.when(pl.program_id(1) == 0)
    def _():
        st_ref[...] = jnp.zeros_like(st_ref)

    z = f_ref[...]
    lb = lb_ref[...]
    log_f = jnp.log2(lb + (1.0 - lb) * jax.nn.sigmoid(z))
    kk = (1.0 - lb) * jax.nn.sigmoid(-z)
    qf = q_ref[...].astype(F32)
    qs = qf * jax.nn.sigmoid(qf) * dk ** -0.5
    hi = log_f.astype(BF16)
    lo = (log_f - hi.astype(F32)).astype(BF16)
    row = lax.broadcasted_iota(jnp.int32, (c, q_ref.shape[1]), 0)

    head_cols = [slice(hd * dk, (hd + 1) * dk) for hd in range(A_HEADS)]
    n_chunks = q_ref.shape[0] // c

    independent = []
    for ci in range(n_chunks):
        rows = slice(ci * c, (ci + 1) * c)
        x = jnp.dot(ww_ref[...], jnp.concatenate([hi[rows], lo[rows]], axis=0),
                    preferred_element_type=F32)
        b = x[0:c]
        chunk_decay = jnp.exp2(b[c - 1:c, :])
        q_c, k_c, v = qs[rows], kk[rows], i_ref[rows, :]
        q_in = (q_c * jnp.exp2(b)).astype(BF16)
        k_out = (k_c * jnp.exp2(x[c:2 * c])).astype(BF16)
        sides = [(q_c.astype(BF16), k_c.astype(BF16))]
        for level in range(HGRN_LEVELS):
            half = (c >> level) // 2
            e = jnp.exp2(x[(2 + level) * c:(3 + level) * c])
            m = (jnp.where((row & half) != 0, q_c, k_c) * e).astype(BF16)
            sides.append((m, m))
        o_intra, upd_t = [], []
        for cols in head_cols:
            a = sum(lax.dot_general(lhs[:, cols], rhs[:, cols], _NT,
                                    preferred_element_type=F32) * mask_ref[n]
                    for n, (lhs, rhs) in enumerate(sides))
            o_intra.append(jnp.dot(a.astype(BF16), v[:, cols], preferred_element_type=F32))
            upd_t.append(lax.dot_general(v[:, cols], k_out[:, cols], _TN,
                                         preferred_element_type=F32))
        independent.append((q_in, chunk_decay, o_intra, upd_t))

    states = [st_ref[hd] for hd in range(A_HEADS)]
    for ci, (q_in, chunk_decay, o_intra, upd_t) in enumerate(independent):
        outs = []
        for hd, cols in enumerate(head_cols):
            o = o_intra[hd] + lax.dot_general(q_in[:, cols], states[hd].astype(BF16), _NT,
                                              preferred_element_type=F32)
            states[hd] = states[hd] * chunk_decay[:, cols] + upd_t[hd]
            outs.append(_rms_norm(o, gain_ref[:, cols]))
        rows = slice(ci * c, (ci + 1) * c)
        gf = g_ref[rows, :].astype(F32)
        o_ref[rows, :] = (jnp.concatenate(outs, axis=-1)
                          * (gf * jax.nn.sigmoid(gf))).astype(o_ref.dtype)
    for hd in range(A_HEADS):
        st_ref[hd] = states[hd]


HGRN_GROUP = 4


def hgrn2_mixer(q, f, i, g, lower_bound, out_gain, batch, seq):
    rows, width = q.shape
    n_chunks = seq // (HGRN_GROUP * CHUNK)
    ww, masks = _hgrn_tables()
    act = pl.BlockSpec((HGRN_GROUP * CHUNK, width), lambda b, n: (b * n_chunks + n, 0))
    return pl.pallas_call(
        _hgrn_kernel,
        grid=(batch, n_chunks),
        in_specs=[act, act, act, act, _resident((1, width)), _resident((1, width)),
                  _resident(ww.shape), _resident(masks.shape)],
        out_specs=act,
        out_shape=jax.ShapeDtypeStruct((rows, width), BF16),
        scratch_shapes=[pltpu.VMEM((A_HEADS, A_HEAD_DIM, A_HEAD_DIM), F32)],
        compiler_params=_params("parallel", "arbitrary"),
        name="hgrn2",
    )(q, f, i, g, lower_bound.reshape(1, width), out_gain.reshape(1, width),
      jnp.asarray(ww, BF16), jnp.asarray(masks))


BAND_GROUP = 4
BAND_ROWS = BAND_GROUP * CHUNK
BAND_WINDOW = (BAND_GROUP + B_PREV_CHUNKS) * CHUNK
BIAS_SPAN = BAND_ROWS + BAND_WINDOW


def _rel_bias_kernel(table_ref, o_ref):
    heads, n_idx = table_ref.shape
    e = lax.broadcasted_iota(jnp.int32, (n_idx, BIAS_SPAN), 1)
    distance = B_PREV_CHUNKS * CHUNK + BAND_ROWS - 1 - e
    idx = jnp.clip(distance, -REL_CLIP, REL_CLIP) + REL_CLIP
    onehot = (lax.broadcasted_iota(jnp.int32, (n_idx, BIAS_SPAN), 0) == idx).astype(F32)
    by_offset = jnp.dot(table_ref[...], onehot, preferred_element_type=F32,
                        precision=lax.Precision.HIGHEST) * LOG2_E
    col = lax.broadcasted_iota(jnp.int32, (heads, BAND_WINDOW), 1)
    for r in range(BAND_ROWS):
        band_start = (r // CHUNK) * CHUNK
        in_band = (col >= band_start) & (col < band_start + B_BAND * CHUNK)
        shift = BAND_ROWS - 1 - r
        o_ref[:, r * BAND_WINDOW:(r + 1) * BAND_WINDOW] = jnp.where(
            in_band, by_offset[:, shift:shift + BAND_WINDOW], -jnp.inf)


def rel_bias(rel_table):
    heads, n_rel = rel_table.shape
    n_idx = -(-n_rel // LANES) * LANES
    table = jnp.pad(rel_table.astype(F32), ((0, 0), (0, n_idx - n_rel)))
    out = pl.pallas_call(
        _rel_bias_kernel,
        in_specs=[pl.BlockSpec(memory_space=pltpu.VMEM)],
        out_specs=pl.BlockSpec(memory_space=pltpu.VMEM),
        out_shape=jax.ShapeDtypeStruct((heads, BAND_ROWS * BAND_WINDOW), F32),
        compiler_params=pltpu.CompilerParams(vmem_limit_bytes=VMEM_LIMIT_BYTES),
        name="rel_bias",
    )(table)
    return out.reshape(heads, BAND_ROWS, BAND_WINDOW)


def _band_attn_kernel(q_ref, k_ref, v_ref, bias_ref, o_ref):
    g = pl.program_id(1)
    c, dh = CHUNK, B_HEAD_DIM
    n_slots = BAND_WINDOW // c
    first = g * BAND_GROUP - B_PREV_CHUNKS
    lane = lax.broadcasted_iota(jnp.int32, (BAND_ROWS, 2 * dh), 1)
    lane_kv = lax.broadcasted_iota(jnp.int32, (BAND_WINDOW, 2 * dh), 1)

    def attend(at_sequence_start):
        starts = [pl.multiple_of(jnp.maximum(first + j, 0) * c, c) for j in range(n_slots)]
        k_win = jnp.concatenate([k_ref[0, pl.ds(s, c), :] for s in starts], axis=0)
        v_win = jnp.concatenate([v_ref[0, pl.ds(s, c), :] for s in starts], axis=0)
        started = lax.broadcasted_iota(jnp.int32, (BAND_ROWS, BAND_WINDOW), 1) >= -first * c
        q_all = q_ref[...]
        for pair in range(B_HEADS // 2):
            cols = slice(pair * 2 * dh, (pair + 1) * 2 * dh)
            q_pair, k_pair, v_pair = q_all[:, cols], k_win[:, cols], v_win[:, cols]
            q_both = jnp.concatenate([jnp.where(lane < dh, q_pair, jnp.zeros_like(q_pair)),
                                      jnp.where(lane >= dh, q_pair, jnp.zeros_like(q_pair))],
                                     axis=0)
            s_both = lax.dot_general(q_both, k_pair, _NT, preferred_element_type=F32)
            outs = []
            for sub in range(2):
                s = s_both[sub * BAND_ROWS:(sub + 1) * BAND_ROWS] + bias_ref[2 * pair + sub]
                if at_sequence_start:
                    s = jnp.where(started, s, -jnp.inf)
                p = jnp.exp2(s - jnp.max(s, axis=-1, keepdims=True)).astype(BF16)
                ones_lane = dh if sub == 0 else 0
                own_kv = (lane_kv < dh) if sub == 0 else (lane_kv >= dh)
                v_h = jnp.where(own_kv, v_pair,
                                jnp.where(lane_kv == ones_lane, 1.0, 0.0).astype(v_pair.dtype))
                o = jnp.dot(p, v_h, preferred_element_type=F32)
                outs.append(o / o[:, ones_lane:ones_lane + 1])
            o_ref[:, cols] = jnp.where(lane < dh, outs[0], outs[1]).astype(o_ref.dtype)

    pl.when(first < 0)(functools.partial(attend, True))
    pl.when(first >= 0)(functools.partial(attend, False))


def band_attention(q, k, v, bias, batch, seq):
    rows, width = q.shape
    n_groups = seq // BAND_ROWS
    kv_spec = pl.BlockSpec((1, seq, width), lambda b, g: (b, 0, 0))
    return pl.pallas_call(
        _band_attn_kernel,
        grid=(batch, n_groups),
        in_specs=[pl.BlockSpec((BAND_ROWS, width), lambda b, g: (b * n_groups + g, 0)),
                  kv_spec, kv_spec, _resident(bias.shape)],
        out_specs=pl.BlockSpec((BAND_ROWS, width), lambda b, g: (b * n_groups + g, 0)),
        out_shape=jax.ShapeDtypeStruct((rows, width), BF16),
        compiler_params=_params("parallel", "arbitrary"),
        name="band_attn",
    )(q, k.reshape(batch, seq, width), v.reshape(batch, seq, width), bias)


def _fox_cum_kernel(blk, f_ref, bias_ref, tril_ref, o_ref):
    seq = f_ref.shape[1]
    carry = jnp.zeros((1, f_ref.shape[2]), F32)
    for c in range(seq // blk):
        x = f_ref[0, c * blk:(c + 1) * blk, :] + bias_ref[...]
        log_sig = jnp.minimum(x, 0.0) - jnp.log1p(jnp.exp(-jnp.abs(x)))
        cs = jnp.dot(tril_ref[...], log_sig, preferred_element_type=F32,
                     precision=lax.Precision.HIGHEST) + carry
        carry = cs[blk - 1:blk, :]
        o_ref[0, c * blk:(c + 1) * blk, :] = (cs * LOG2_E)[:, :o_ref.shape[2]]


def fox_cum(f, f_bias, batch, seq, blk=256):
    width = f.shape[-1]
    n_gates = f_bias.shape[0]
    bias = jnp.pad(f_bias.astype(F32), (0, width - n_gates)).reshape(1, width)
    tril = jnp.asarray(np.tril(np.ones((blk, blk), np.float32)))
    return pl.pallas_call(
        functools.partial(_fox_cum_kernel, blk),
        grid=(batch,),
        in_specs=[pl.BlockSpec((1, seq, width), lambda b: (b, 0, 0)), _resident((1, width)),
                  _resident((blk, blk))],
        out_specs=pl.BlockSpec((1, seq, n_gates), lambda b: (b, 0, 0)),
        out_shape=jax.ShapeDtypeStruct((batch, seq, n_gates), F32),
        compiler_params=_params("parallel"),
        name="fox_cum",
    )(f.reshape(batch, seq, width), bias, tril)


def _fox_attn_kernel(blk, q_ref, k_ref, v_ref, cum_ref, o_ref):
    step = pl.program_id(2)
    dh = C_HEAD_DIM
    n_blk = q_ref.shape[1] // blk
    lane = lax.broadcasted_iota(jnp.int32, (blk, 2 * dh), 1)
    key_le_query = (lax.broadcasted_iota(jnp.int32, (blk, blk), 0)
                    <= lax.broadcasted_iota(jnp.int32, (blk, blk), 1))
    row_o = lax.broadcasted_iota(jnp.int32, (2 * dh, blk), 0)

    def attend(n_past):
        past = n_past * blk
        row_v = lax.broadcasted_iota(jnp.int32, (2 * dh, past + blk), 0)
        for pair in range(q_ref.shape[2] // (2 * dh)):
            cols = slice(pair * 2 * dh, (pair + 1) * 2 * dh)
            q_pair = q_ref[0, past:past + blk, cols]
            k_all, v_all = k_ref[0, :past + blk, cols], v_ref[0, :past + blk, cols]
            q_both = jnp.concatenate([jnp.where(lane < dh, q_pair, jnp.zeros_like(q_pair)),
                                      jnp.where(lane >= dh, q_pair, jnp.zeros_like(q_pair))],
                                     axis=0)
            st_both = lax.dot_general(k_all, q_both, _NT, preferred_element_type=F32)
            v_t = v_all.T
            outs = []
            for sub in range(2):
                gate = 2 * pair + sub
                st = (st_both[:, sub * blk:(sub + 1) * blk]
                      - cum_ref[0, 0, :past + blk, gate:gate + 1])
                parts = [jnp.where(key_le_query, st[past:], -jnp.inf)]
                if n_past:
                    parts.insert(0, st[:past])
                m = functools.reduce(jnp.maximum,
                                     [jnp.max(t, axis=0, keepdims=True) for t in parts])
                p_t = jnp.concatenate([jnp.exp2(t - m).astype(BF16) for t in parts], axis=0)
                ones_row = dh if sub == 0 else 0
                own = (row_v < dh) if sub == 0 else (row_v >= dh)
                v_h_t = jnp.where(own, v_t,
                                  jnp.where(row_v == ones_row, 1.0, 0.0).astype(v_t.dtype))
                o_t = jnp.dot(v_h_t, p_t, preferred_element_type=F32)
                outs.append(o_t / o_t[ones_row:ones_row + 1, :])
            o_pair_t = jnp.where(row_o < dh, outs[0], outs[1])
            o_ref[0, past:past + blk, cols] = o_pair_t.T.astype(o_ref.dtype)

    def attend_both(i):
        attend(i)
        attend(n_blk - 1 - i)

    for i in range(n_blk // 2):
        pl.when(step == i)(functools.partial(attend_both, i))


FOX_PAIRS_PER_STEP = 2


def fox_attention(q, k, v, cum_t, batch, seq, blk=FOX_BLOCK):
    width = q.shape[-1]
    pairs = C_HEADS // 2 // FOX_PAIRS_PER_STEP
    pair_w = 2 * C_HEAD_DIM * FOX_PAIRS_PER_STEP
    gates = 2 * FOX_PAIRS_PER_STEP
    cum = cum_t.reshape(batch, seq, C_HEADS // gates, gates).transpose(0, 2, 1, 3)
    q3, k3, v3 = (t.reshape(batch, seq, width) for t in (q, k, v))
    seq_spec = pl.BlockSpec((1, seq, pair_w), lambda b, p, i: (b, 0, p))
    assert (seq // blk) % 2 == 0
    out = pl.pallas_call(
        functools.partial(_fox_attn_kernel, blk),
        grid=(batch, pairs, seq // blk // 2),
        in_specs=[seq_spec, seq_spec, seq_spec,
                  pl.BlockSpec((1, 1, seq, gates), lambda b, p, i: (b, p, 0, 0))],
        out_specs=seq_spec,
        out_shape=jax.ShapeDtypeStruct((batch, seq, width), BF16),
        compiler_params=_params("parallel", "parallel", "arbitrary"),
        name="fox_attn",
    )(q3, k3, v3, cum)
    return out.reshape(batch * seq, width)


def _even_mixer(h, gain, w_in_stack, e, lower_bound, out_gain, rel_table, batch, seq):
    a_w, b_w = A_HEADS * A_HEAD_DIM, B_HEADS * B_HEAD_DIM
    splits = [(a_w, BF16, 1.0), (a_w, F32, 1.0), (a_w, BF16, 1.0), (a_w, BF16, 1.0),
              (b_w, BF16, B_HEAD_DIM ** -0.5 * LOG2_E), (b_w, BF16, 1.0), (b_w, BF16, 1.0)]
    q_a, f_a, i_a, g_a, q_b, k_b, v_b = norm_proj(h, gain, w_in_stack, e, splits)
    o_a = hgrn2_mixer(q_a, f_a, i_a, g_a, lower_bound, out_gain, batch, seq)
    o_b = band_attention(q_b, k_b, v_b, rel_bias(rel_table), batch, seq)
    return [o_a, o_b]


def _pad_gate_columns(w_in_c):
    return jnp.pad(w_in_c, ((0, 0), (0, 0), (0, LANES - C_HEADS)))


def _odd_mixer(h, gain, w_in_stack, o, f_bias, batch, seq):
    c_w = C_HEADS * C_HEAD_DIM
    splits = [(c_w, BF16, C_HEAD_DIM ** -0.5 * LOG2_E), (c_w, BF16, 1.0), (c_w, BF16, 1.0),
              (LANES, F32, 1.0)]
    q, k, v, f = norm_proj(h, gain, w_in_stack, o, splits)
    cum_t = fox_cum(f, f_bias, batch, seq)
    return [fox_attention(q, k, v, cum_t, batch, seq)]


def kernel(x, mem, norm_mix, norm_xattn, norm_mem, norm_mlp, norm_final, w_in_ab, a_lb_logits,
           a_out_gain, b_rel_bias, w_out_ab, w_in_c, c_fgate_bias, w_out_c, w_xq, w_xkv, w_xo,
           w_up, w_down):
    batch, seq, d = x.shape
    mem_tokens = mem.shape[1]
    depth = norm_mix.shape[0]
    lb_w = jax.nn.softmax(a_lb_logits.astype(F32), axis=0)
    lower_bounds = jnp.cumsum(lb_w, axis=0) - lb_w[0]
    h = x.reshape(batch * seq, d)
    mem2 = mem.reshape(batch * mem_tokens, d)
    w_in_ab, w_in_c = w_in_ab.astype(BF16), _pad_gate_columns(w_in_c).astype(BF16)
    w_out_ab, w_out_c = w_out_ab.astype(BF16), w_out_c.astype(BF16)
    w_xq, w_xkv, w_xo = w_xq.astype(BF16), w_xkv.astype(BF16), w_xo.astype(BF16)
    w_up, w_down = w_up.astype(BF16), w_down.astype(BF16)
    for layer in range(depth):
        mixer = layer // 2
        if layer % 2 == 0:
            acts = _even_mixer(h, norm_mix[layer], w_in_ab, mixer, lower_bounds[mixer],
                               a_out_gain[mixer], b_rel_bias[mixer], batch, seq)
            w_out = w_out_ab
        else:
            acts = _odd_mixer(h, norm_mix[layer], w_in_c, mixer, c_fgate_bias[mixer], batch, seq)
            w_out = w_out_c
        k_mem, v_mem = norm_proj(mem2, norm_mem[layer], w_xkv, layer,
                                 [(d, BF16, 1.0), (d, BF16, 1.0)])
        h = layer_tail(h, acts, w_out, mixer, norm_xattn[layer], w_xq,
                       k_mem.reshape(batch, mem_tokens, d), v_mem.reshape(batch, mem_tokens, d),
                       w_xo, norm_mlp[layer], w_up, w_down, layer, seq,
                       g_final=norm_final if layer == depth - 1 else None)
    return h.reshape(batch, seq, d)
```

```python
import functools

import numpy as np
import jax
import jax.numpy as jnp
from jax import lax
from jax.experimental import pallas as pl
from jax.experimental.pallas import tpu as pltpu

F32 = jnp.float32
BF16 = jnp.bfloat16

EPS = 1e-6
LOG2_E = 1.4426950408889634
CHUNK = 64
A_HEADS = 4
A_HEAD_DIM = 128
B_HEADS = 8
B_HEAD_DIM = 64
B_PREV_CHUNKS = 8
B_BAND = B_PREV_CHUNKS + 1
REL_CLIP = 2 * CHUNK
C_HEADS = 16
C_HEAD_DIM = 64
X_HEADS = 4
HGRN_LEVELS = 6

LANES = 128
SUBLANES = 8
VMEM_LIMIT_BYTES = 48 * 1024 * 1024
TAIL_VMEM_LIMIT_BYTES = 56 * 1024 * 1024

ROW_TILE = 512
FOX_BLOCK = 512

_NT = (((1,), (1,)), ((), ()))
_TN = (((0,), (0,)), ((), ()))


def _params(*semantics):
    return pltpu.CompilerParams(dimension_semantics=semantics, vmem_limit_bytes=VMEM_LIMIT_BYTES)


def _rms_norm(x, gain):
    return x * lax.rsqrt(jnp.mean(x * x, axis=-1, keepdims=True) + EPS) * gain


def _resident(shape):
    return pl.BlockSpec(shape, lambda *_: (0,) * len(shape), pipeline_mode=pl.Buffered(1))


def _resident_layer(stack_shape, layer):
    return pl.BlockSpec((None,) + tuple(stack_shape[1:]),
                        lambda *_: (layer,) + (0,) * (len(stack_shape) - 1),
                        pipeline_mode=pl.Buffered(1))


def _norm_proj_kernel(scales, x_ref, g_ref, w_ref, *out_refs):
    xn = _rms_norm(x_ref[...], g_ref[...]).astype(BF16)
    off = 0
    for o_ref, scale in zip(out_refs, scales):
        width = o_ref.shape[-1]
        y = jnp.dot(xn, w_ref[:, off:off + width], preferred_element_type=F32)
        o_ref[...] = (y if scale == 1.0 else y * scale).astype(o_ref.dtype)
        off += width


def norm_proj(x, gain, w_stack, layer, splits, row_tile=ROW_TILE):
    rows, d = x.shape
    n = w_stack.shape[2]
    assert sum(width for width, _, _ in splits) == n and rows % row_tile == 0
    return pl.pallas_call(
        functools.partial(_norm_proj_kernel, tuple(scale for _, _, scale in splits)),
        grid=(rows // row_tile,),
        in_specs=[pl.BlockSpec((row_tile, d), lambda i: (i, 0)),
                  _resident((1, d)), _resident_layer(w_stack.shape, layer)],
        out_specs=[pl.BlockSpec((row_tile, width), lambda i: (i, 0)) for width, _, _ in splits],
        out_shape=[jax.ShapeDtypeStruct((rows, width), dt) for width, dt, _ in splits],
        compiler_params=_params("parallel"),
        name="norm_proj",
    )(x, gain.reshape(1, d), w_stack)


def _xattn_update(h, gain, wq_ref, k_ref, v_ref, wo_ref):
    dh = h.shape[-1] // X_HEADS
    xn = _rms_norm(h, gain).astype(BF16)
    q = (jnp.dot(xn, wq_ref[...], preferred_element_type=F32) * (dh ** -0.5 * LOG2_E)).astype(BF16)
    outs = []
    for hd in range(X_HEADS):
        cols = slice(hd * dh, (hd + 1) * dh)
        s = lax.dot_general(q[:, cols], k_ref[0, :, cols], _NT, preferred_element_type=F32)
        p = jnp.exp2(s - jnp.max(s, axis=-1, keepdims=True))
        denom = jnp.sum(p, axis=-1, keepdims=True)
        o = jnp.dot(p.astype(BF16), v_ref[0, :, cols], preferred_element_type=F32)
        outs.append((o / denom).astype(BF16))
    return h + jnp.dot(jnp.concatenate(outs, axis=-1), wo_ref[...], preferred_element_type=F32)


def _mlp_update(h, gain, wu_ref, wd_ref, ff_chunk):
    xn = _rms_norm(h, gain).astype(BF16)
    acc = h
    for c in range(wu_ref.shape[1] // ff_chunk):
        cols = slice(c * ff_chunk, (c + 1) * ff_chunk)
        u = jnp.maximum(jnp.dot(xn, wu_ref[:, cols], preferred_element_type=F32), 0.0)
        acc = acc + jnp.dot((u * u).astype(BF16), wd_ref[cols, :], preferred_element_type=F32)
    return acc


def _layer_tail_kernel(n_act, final, ff_chunk, h_ref, *refs):
    a_refs = refs[:n_act]
    (wout_ref, gx_ref, wq_ref, k_ref, v_ref, wo_ref, gm_ref, wu_ref,
     wd_ref) = refs[n_act:n_act + 9]
    o_ref = refs[-1]
    h = h_ref[...]
    off = 0
    for a_ref in a_refs:
        width = a_ref.shape[1]
        h = h + jnp.dot(a_ref[...], wout_ref[off:off + width, :], preferred_element_type=F32)
        off += width
    h = _xattn_update(h, gx_ref[...], wq_ref, k_ref, v_ref, wo_ref)
    h = _mlp_update(h, gm_ref[...], wu_ref, wd_ref, ff_chunk)
    if final:
        h = _rms_norm(h, refs[-2][...])
    o_ref[...] = h


def layer_tail(h, acts, w_out, mixer, g_xattn, w_q, k_mem, v_mem, w_o, g_mlp, w_up, w_down, layer,
               seq, g_final=None, row_tile=ROW_TILE, ff_chunk=1024):
    rows, d = h.shape
    tiles_per_seq = seq // row_tile
    row_spec = lambda width: pl.BlockSpec((row_tile, width), lambda i: (i, 0))
    mem_spec = pl.BlockSpec((1,) + k_mem.shape[1:], lambda i: (i // tiles_per_seq, 0, 0))
    gains = [g.reshape(1, d) for g in (g_xattn, g_mlp)]
    operands = [h, *acts, w_out, gains[0], w_q, k_mem, v_mem, w_o, gains[1], w_up, w_down]
    in_specs = ([row_spec(d)] + [row_spec(a.shape[1]) for a in acts]
                + [_resident_layer(w_out.shape, mixer), _resident((1, d)),
                   _resident_layer(w_q.shape, layer), mem_spec, mem_spec,
                   _resident_layer(w_o.shape, layer), _resident((1, d)),
                   _resident_layer(w_up.shape, layer), _resident_layer(w_down.shape, layer)])
    if g_final is not None:
        operands.append(g_final.reshape(1, d))
        in_specs.append(_resident((1, d)))
    return pl.pallas_call(
        functools.partial(_layer_tail_kernel, len(acts), g_final is not None, ff_chunk),
        grid=(rows // row_tile,),
        in_specs=in_specs,
        out_specs=row_spec(d),
        out_shape=jax.ShapeDtypeStruct((rows, d), F32),
        compiler_params=pltpu.CompilerParams(dimension_semantics=("parallel",),
                                             vmem_limit_bytes=TAIL_VMEM_LIMIT_BYTES),
        name="layer_tail",
    )(*operands)


def _hgrn_tables():
    c = CHUNK
    t = np.arange(c)[:, None]
    j = np.arange(c)[None, :]
    blocks = [(j <= t), (j > t)]
    masks = [np.eye(c, dtype=bool)]
    for level in range(HGRN_LEVELS):
        size = c >> level
        half = size // 2
        ref = (t // size) * size + half - 1
        upper = (t % size) >= half
        blocks.append(np.where(upper, (j > ref) & (j <= t), (j > t) & (j <= ref)))
        tt, ss = np.arange(c)[:, None], np.arange(c)[None, :]
        masks.append((tt // size == ss // size) & ((tt % size) >= half) & ((ss % size) < half))
    w = np.concatenate(blocks, axis=0).astype(np.float32)
    return np.concatenate([w, w], axis=1), np.stack(masks).astype(np.float32)


def _hgrn_kernel(q_ref, f_ref, i_ref, g_ref, lb_ref, gain_ref, ww_ref, mask_ref, o_ref, st_ref):
    c, dk = CHUNK, A_HEAD_DIM

    @pl.when(pl.program_id(1) == 0)
    def _():
        st_ref[...] = jnp.zeros_like(st_ref)

    z = f_ref[...]
    lb = lb_ref[...]
    log_f = jnp.log2(lb + (1.0 - lb) * jax.nn.sigmoid(z))
    kk = (1.0 - lb) * jax.nn.sigmoid(-z)
    qf = q_ref[...].astype(F32)
    qs = qf * jax.nn.sigmoid(qf) * dk ** -0.5
    hi = log_f.astype(BF16)
    lo = (log_f - hi.astype(F32)).astype(BF16)
    row = lax.broadcasted_iota(jnp.int32, (c, q_ref.shape[1]), 0)

    head_cols = [slice(hd * dk, (hd + 1) * dk) for hd in range(A_HEADS)]
    n_chunks = q_ref.shape[0] // c

    independent = []
    for ci in range(n_chunks):
        rows = slice(ci * c, (ci + 1) * c)
        x = jnp.dot(ww_ref[...], jnp.concatenate([hi[rows], lo[rows]], axis=0),
                    preferred_element_type=F32)
        b = x[0:c]
        chunk_decay = jnp.exp2(b[c - 1:c, :])
        q_c, k_c, v = qs[rows], kk[rows], i_ref[rows, :]
        q_in = (q_c * jnp.exp2(b)).astype(BF16)
        k_out = (k_c * jnp.exp2(x[c:2 * c])).astype(BF16)
        sides = [(q_c.astype(BF16), k_c.astype(BF16))]
        for level in range(HGRN_LEVELS):
            half = (c >> level) // 2
            e = jnp.exp2(x[(2 + level) * c:(3 + level) * c])
            m = (jnp.where((row & half) != 0, q_c, k_c) * e).astype(BF16)
            sides.append((m, m))
        o_intra, upd_t = [], []
        for cols in head_cols:
            a = sum(lax.dot_general(lhs[:, cols], rhs[:, cols], _NT,
                                    preferred_element_type=F32) * mask_ref[n]
                    for n, (lhs, rhs) in enumerate(sides))
            o_intra.append(jnp.dot(a.astype(BF16), v[:, cols], preferred_element_type=F32))
            upd_t.append(lax.dot_general(v[:, cols], k_out[:, cols], _TN,
                                         preferred_element_type=F32))
        independent.append((q_in, chunk_decay, o_intra, upd_t))

    states = [st_ref[hd] for hd in range(A_HEADS)]
    for ci, (q_in, chunk_decay, o_intra, upd_t) in enumerate(independent):
        outs = []
        for hd, cols in enumerate(head_cols):
            o = o_intra[hd] + lax.dot_general(q_in[:, cols], states[hd].astype(BF16), _NT,
                                              preferred_element_type=F32)
            states[hd] = states[hd] * chunk_decay[:, cols] + upd_t[hd]
            outs.append(_rms_norm(o, gain_ref[:, cols]))
        rows = slice(ci * c, (ci + 1) * c)
        gf = g_ref[rows, :].astype(F32)
        o_ref[rows, :] = (jnp.concatenate(outs, axis=-1)
                          * (gf * jax.nn.sigmoid(gf))).astype(o_ref.dtype)
    for hd in range(A_HEADS):
        st_ref[hd] = states[hd]


HGRN_GROUP = 8


def hgrn2_mixer(q, f, i, g, lower_bound, out_gain, batch, seq):
    rows, width = q.shape
    n_chunks = seq // (HGRN_GROUP * CHUNK)
    ww, masks = _hgrn_tables()
    act = pl.BlockSpec((HGRN_GROUP * CHUNK, width), lambda b, n: (b * n_chunks + n, 0))
    return pl.pallas_call(
        _hgrn_kernel,
        grid=(batch, n_chunks),
        in_specs=[act, act, act, act, _resident((1, width)), _resident((1, width)),
                  _resident(ww.shape), _resident(masks.shape)],
        out_specs=act,
        out_shape=jax.ShapeDtypeStruct((rows, width), BF16),
        scratch_shapes=[pltpu.VMEM((A_HEADS, A_HEAD_DIM, A_HEAD_DIM), F32)],
        compiler_params=_params("parallel", "arbitrary"),
        name="hgrn2",
    )(q, f, i, g, lower_bound.reshape(1, width), out_gain.reshape(1, width),
      jnp.asarray(ww, BF16), jnp.asarray(masks))


BAND_GROUP = 4
BAND_ROWS = BAND_GROUP * CHUNK
BAND_WINDOW = (BAND_GROUP + B_PREV_CHUNKS) * CHUNK
BIAS_SPAN = BAND_ROWS + BAND_WINDOW


def _rel_bias_kernel(table_ref, o_ref):
    heads, n_idx = table_ref.shape
    e = lax.broadcasted_iota(jnp.int32, (n_idx, BIAS_SPAN), 1)
    distance = B_PREV_CHUNKS * CHUNK + BAND_ROWS - 1 - e
    idx = jnp.clip(distance, -REL_CLIP, REL_CLIP) + REL_CLIP
    onehot = (lax.broadcasted_iota(jnp.int32, (n_idx, BIAS_SPAN), 0) == idx).astype(F32)
    by_offset = jnp.dot(table_ref[...], onehot, preferred_element_type=F32,
                        precision=lax.Precision.HIGHEST) * LOG2_E
    col = lax.broadcasted_iota(jnp.int32, (heads, BAND_WINDOW), 1)
    for r in range(BAND_ROWS):
        band_start = (r // CHUNK) * CHUNK
        in_band = (col >= band_start) & (col < band_start + B_BAND * CHUNK)
        shift = BAND_ROWS - 1 - r
        o_ref[:, r * BAND_WINDOW:(r + 1) * BAND_WINDOW] = jnp.where(
            in_band, by_offset[:, shift:shift + BAND_WINDOW], -jnp.inf)


def rel_bias(rel_table):
    heads, n_rel = rel_table.shape
    n_idx = -(-n_rel // LANES) * LANES
    table = jnp.pad(rel_table.astype(F32), ((0, 0), (0, n_idx - n_rel)))
    out = pl.pallas_call(
        _rel_bias_kernel,
        in_specs=[pl.BlockSpec(memory_space=pltpu.VMEM)],
        out_specs=pl.BlockSpec(memory_space=pltpu.VMEM),
        out_shape=jax.ShapeDtypeStruct((heads, BAND_ROWS * BAND_WINDOW), F32),
        compiler_params=pltpu.CompilerParams(vmem_limit_bytes=VMEM_LIMIT_BYTES),
        name="rel_bias",
    )(table)
    return out.reshape(heads, BAND_ROWS, BAND_WINDOW)


def _band_attn_kernel(q_ref, k_ref, v_ref, bias_ref, o_ref):
    g = pl.program_id(1)
    c, dh = CHUNK, B_HEAD_DIM
    n_slots = BAND_WINDOW // c
    first = g * BAND_GROUP - B_PREV_CHUNKS
    lane = lax.broadcasted_iota(jnp.int32, (BAND_ROWS, 2 * dh), 1)
    lane_kv = lax.broadcasted_iota(jnp.int32, (BAND_WINDOW, 2 * dh), 1)

    def attend(at_sequence_start):
        starts = [pl.multiple_of(jnp.maximum(first + j, 0) * c, c) for j in range(n_slots)]
        k_win = jnp.concatenate([k_ref[0, pl.ds(s, c), :] for s in starts], axis=0)
        v_win = jnp.concatenate([v_ref[0, pl.ds(s, c), :] for s in starts], axis=0)
        started = lax.broadcasted_iota(jnp.int32, (BAND_ROWS, BAND_WINDOW), 1) >= -first * c
        q_all = q_ref[...]
        for pair in range(B_HEADS // 2):
            cols = slice(pair * 2 * dh, (pair + 1) * 2 * dh)
            q_pair, k_pair, v_pair = q_all[:, cols], k_win[:, cols], v_win[:, cols]
            q_both = jnp.concatenate([jnp.where(lane < dh, q_pair, jnp.zeros_like(q_pair)),
                                      jnp.where(lane >= dh, q_pair, jnp.zeros_like(q_pair))],
                                     axis=0)
            s_both = lax.dot_general(q_both, k_pair, _NT, preferred_element_type=F32)
            outs = []
            for sub in range(2):
                s = s_both[sub * BAND_ROWS:(sub + 1) * BAND_ROWS] + bias_ref[2 * pair + sub]
                if at_sequence_start:
                    s = jnp.where(started, s, -jnp.inf)
                p = jnp.exp2(s - jnp.max(s, axis=-1, keepdims=True)).astype(BF16)
                ones_lane = dh if sub == 0 else 0
                own_kv = (lane_kv < dh) if sub == 0 else (lane_kv >= dh)
                v_h = jnp.where(own_kv, v_pair,
                                jnp.where(lane_kv == ones_lane, 1.0, 0.0).astype(v_pair.dtype))
                o = jnp.dot(p, v_h, preferred_element_type=F32)
                outs.append(o / o[:, ones_lane:ones_lane + 1])
            o_ref[:, cols] = jnp.where(lane < dh, outs[0], outs[1]).astype(o_ref.dtype)

    pl.when(first < 0)(functools.partial(attend, True))
    pl.when(first >= 0)(functools.partial(attend, False))


def band_attention(q, k, v, bias, batch, seq):
    rows, width = q.shape
    n_groups = seq // BAND_ROWS
    kv_spec = pl.BlockSpec((1, seq, width), lambda b, g: (b, 0, 0))
    return pl.pallas_call(
        _band_attn_kernel,
        grid=(batch, n_groups),
        in_specs=[pl.BlockSpec((BAND_ROWS, width), lambda b, g: (b * n_groups + g, 0)),
                  kv_spec, kv_spec, _resident(bias.shape)],
        out_specs=pl.BlockSpec((BAND_ROWS, width), lambda b, g: (b * n_groups + g, 0)),
        out_shape=jax.ShapeDtypeStruct((rows, width), BF16),
        compiler_params=_params("parallel", "arbitrary"),
        name="band_attn",
    )(q, k.reshape(batch, seq, width), v.reshape(batch, seq, width), bias)


def _fox_cum_kernel(blk, f_ref, bias_ref, tril_ref, o_ref):
    seq = f_ref.shape[1]
    carry = jnp.zeros((1, f_ref.shape[2]), F32)
    for c in range(seq // blk):
        x = f_ref[0, c * blk:(c + 1) * blk, :] + bias_ref[...]
        log_sig = jnp.minimum(x, 0.0) - jnp.log1p(jnp.exp(-jnp.abs(x)))
        cs = jnp.dot(tril_ref[...], log_sig, preferred_element_type=F32,
                     precision=lax.Precision.HIGHEST) + carry
        carry = cs[blk - 1:blk, :]
        o_ref[0, :, c * blk:(c + 1) * blk] = (cs * LOG2_E).T[:o_ref.shape[1]]


def fox_cum(f, f_bias, batch, seq, blk=256):
    width = f.shape[-1]
    n_gates = f_bias.shape[0]
    bias = jnp.pad(f_bias.astype(F32), (0, width - n_gates)).reshape(1, width)
    tril = jnp.asarray(np.tril(np.ones((blk, blk), np.float32)))
    return pl.pallas_call(
        functools.partial(_fox_cum_kernel, blk),
        grid=(batch,),
        in_specs=[pl.BlockSpec((1, seq, width), lambda b: (b, 0, 0)), _resident((1, width)),
                  _resident((blk, blk))],
        out_specs=pl.BlockSpec((1, n_gates, seq), lambda b: (b, 0, 0)),
        out_shape=jax.ShapeDtypeStruct((batch, n_gates, seq), F32),
        compiler_params=_params("parallel"),
        name="fox_cum",
    )(f.reshape(batch, seq, width), bias, tril)


def _fox_attn_kernel(blk, q_ref, k_ref, v_ref, cum_ref, o_ref):
    step = pl.program_id(2)
    dh = C_HEAD_DIM
    n_blk = q_ref.shape[1] // blk
    lane = lax.broadcasted_iota(jnp.int32, (blk, 2 * dh), 1)
    key_le_query = (lax.broadcasted_iota(jnp.int32, (blk, blk), 0)
                    <= lax.broadcasted_iota(jnp.int32, (blk, blk), 1))
    row_o = lax.broadcasted_iota(jnp.int32, (2 * dh, blk), 0)

    def attend(n_past):
        past = n_past * blk
        row_v = lax.broadcasted_iota(jnp.int32, (2 * dh, past + blk), 0)
        cum_cols = cum_ref[0, 0, :, :past + blk].T
        for pair in range(q_ref.shape[2] // (2 * dh)):
            cols = slice(pair * 2 * dh, (pair + 1) * 2 * dh)
            q_pair = q_ref[0, past:past + blk, cols]
            k_all, v_all = k_ref[0, :past + blk, cols], v_ref[0, :past + blk, cols]
            q_both = jnp.concatenate([jnp.where(lane < dh, q_pair, jnp.zeros_like(q_pair)),
                                      jnp.where(lane >= dh, q_pair, jnp.zeros_like(q_pair))],
                                     axis=0)
            st_both = lax.dot_general(k_all, q_both, _NT, preferred_element_type=F32)
            v_t = v_all.T
            outs = []
            for sub in range(2):
                gate = 2 * pair + sub
                st = st_both[:, sub * blk:(sub + 1) * blk] - cum_cols[:, gate:gate + 1]
                parts = [jnp.where(key_le_query, st[past:], -jnp.inf)]
                if n_past:
                    parts.insert(0, st[:past])
                m = functools.reduce(jnp.maximum,
                                     [jnp.max(t, axis=0, keepdims=True) for t in parts])
                p_t = jnp.concatenate([jnp.exp2(t - m).astype(BF16) for t in parts], axis=0)
                ones_row = dh if sub == 0 else 0
                own = (row_v < dh) if sub == 0 else (row_v >= dh)
                v_h_t = jnp.where(own, v_t,
                                  jnp.where(row_v == ones_row, 1.0, 0.0).astype(v_t.dtype))
                o_t = jnp.dot(v_h_t, p_t, preferred_element_type=F32)
                outs.append(o_t / o_t[ones_row:ones_row + 1, :])
            o_pair_t = jnp.where(row_o < dh, outs[0], outs[1])
            o_ref[0, past:past + blk, cols] = o_pair_t.T.astype(o_ref.dtype)

    def attend_both(i):
        attend(i)
        attend(n_blk - 1 - i)

    for i in range(n_blk // 2):
        pl.when(step == i)(functools.partial(attend_both, i))


FOX_PAIRS_PER_STEP = 2


def fox_attention(q, k, v, cum_t, batch, seq, blk=FOX_BLOCK):
    width = q.shape[-1]
    pairs = C_HEADS // 2 // FOX_PAIRS_PER_STEP
    pair_w = 2 * C_HEAD_DIM * FOX_PAIRS_PER_STEP
    gates = 2 * FOX_PAIRS_PER_STEP
    cum = jnp.pad(cum_t.reshape(batch, C_HEADS // gates, gates, seq),
                  ((0, 0), (0, 0), (0, SUBLANES - gates), (0, 0)))
    q3, k3, v3 = (t.reshape(batch, seq, width) for t in (q, k, v))
    seq_spec = pl.BlockSpec((1, seq, pair_w), lambda b, p, i: (b, 0, p))
    assert (seq // blk) % 2 == 0
    out = pl.pallas_call(
        functools.partial(_fox_attn_kernel, blk),
        grid=(batch, pairs, seq // blk // 2),
        in_specs=[seq_spec, seq_spec, seq_spec,
                  pl.BlockSpec((1, 1, SUBLANES, seq), lambda b, p, i: (b, p, 0, 0))],
        out_specs=seq_spec,
        out_shape=jax.ShapeDtypeStruct((batch, seq, width), BF16),
        compiler_params=_params("parallel", "parallel", "arbitrary"),
        name="fox_attn",
    )(q3, k3, v3, cum)
    return out.reshape(batch * seq, width)


def _even_mixer(h, gain, w_in_stack, e, lower_bound, out_gain, rel_table, batch, seq):
    a_w, b_w = A_HEADS * A_HEAD_DIM, B_HEADS * B_HEAD_DIM
    splits = [(a_w, BF16, 1.0), (a_w, F32, 1.0), (a_w, BF16, 1.0), (a_w, BF16, 1.0),
              (b_w, BF16, B_HEAD_DIM ** -0.5 * LOG2_E), (b_w, BF16, 1.0), (b_w, BF16, 1.0)]
    q_a, f_a, i_a, g_a, q_b, k_b, v_b = norm_proj(h, gain, w_in_stack, e, splits)
    o_a = hgrn2_mixer(q_a, f_a, i_a, g_a, lower_bound, out_gain, batch, seq)
    o_b = band_attention(q_b, k_b, v_b, rel_bias(rel_table), batch, seq)
    return [o_a, o_b]


def _pad_gate_columns(w_in_c):
    return jnp.pad(w_in_c, ((0, 0), (0, 0), (0, LANES - C_HEADS)))


def _odd_mixer(h, gain, w_in_stack, o, f_bias, batch, seq):
    c_w = C_HEADS * C_HEAD_DIM
    splits = [(c_w, BF16, C_HEAD_DIM ** -0.5 * LOG2_E), (c_w, BF16, 1.0), (c_w, BF16, 1.0),
              (LANES, F32, 1.0)]
    q, k, v, f = norm_proj(h, gain, w_in_stack, o, splits)
    cum_t = fox_cum(f, f_bias, batch, seq)
    return [fox_attention(q, k, v, cum_t, batch, seq)]


def kernel(x, mem, norm_mix, norm_xattn, norm_mem, norm_mlp, norm_final, w_in_ab, a_lb_logits,
           a_out_gain, b_rel_bias, w_out_ab, w_in_c, c_fgate_bias, w_out_c, w_xq, w_xkv, w_xo,
           w_up, w_down):
    batch, seq, d = x.shape
    mem_tokens = mem.shape[1]
    depth = norm_mix.shape[0]
    lb_w = jax.nn.softmax(a_lb_logits.astype(F32), axis=0)
    lower_bounds = jnp.cumsum(lb_w, axis=0) - lb_w[0]
    h = x.reshape(batch * seq, d)
    mem2 = mem.reshape(batch * mem_tokens, d)
    w_in_ab, w_in_c = w_in_ab.astype(BF16), _pad_gate_columns(w_in_c).astype(BF16)
    w_out_ab, w_out_c = w_out_ab.astype(BF16), w_out_c.astype(BF16)
    w_xq, w_xkv, w_xo = w_xq.astype(BF16), w_xkv.astype(BF16), w_xo.astype(BF16)
    w_up, w_down = w_up.astype(BF16), w_down.astype(BF16)
    for layer in range(depth):
        mixer = layer // 2
        if layer % 2 == 0:
            acts = _even_mixer(h, norm_mix[layer], w_in_ab, mixer, lower_bounds[mixer],
                               a_out_gain[mixer], b_rel_bias[mixer], batch, seq)
            w_out = w_out_ab
        else:
            acts = _odd_mixer(h, norm_mix[layer], w_in_c, mixer, c_fgate_bias[mixer], batch, seq)
            w_out = w_out_c
        k_mem, v_mem = norm_proj(mem2, norm_mem[layer], w_xkv, layer,
                                 [(d, BF16, 1.0), (d, BF16, 1.0)])
        h = layer_tail(h, acts, w_out, mixer, norm_xattn[layer], w_xq,
                       k_mem.reshape(batch, mem_tokens, d), v_mem.reshape(batch, mem_tokens, d),
                       w_xo, norm_mlp[layer], w_up, w_down, layer, seq,
                       g_final=norm_final if layer == depth - 1 else None)
    return h.reshape(batch, seq, d)
```

```python
import functools

import numpy as np
import jax
import jax.numpy as jnp
from jax import lax
from jax.experimental import pallas as pl
from jax.experimental.pallas import tpu as pltpu

F32 = jnp.float32
BF16 = jnp.bfloat16

EPS = 1e-6
LOG2_E = 1.4426950408889634
CHUNK = 64
A_HEADS = 4
A_HEAD_DIM = 128
B_HEADS = 8
B_HEAD_DIM = 64
B_PREV_CHUNKS = 8
B_BAND = B_PREV_CHUNKS + 1
REL_CLIP = 2 * CHUNK
C_HEADS = 16
C_HEAD_DIM = 64
X_HEADS = 4
HGRN_LEVELS = 6

LANES = 128
SUBLANES = 8
VMEM_LIMIT_BYTES = 48 * 1024 * 1024
TAIL_VMEM_LIMIT_BYTES = 56 * 1024 * 1024

ROW_TILE = 512
FOX_BLOCK = 512

_NT = (((1,), (1,)), ((), ()))
_TN = (((0,), (0,)), ((), ()))


def _params(*semantics):
    return pltpu.CompilerParams(dimension_semantics=semantics, vmem_limit_bytes=VMEM_LIMIT_BYTES)


def _rms_norm(x, gain):
    return x * lax.rsqrt(jnp.mean(x * x, axis=-1, keepdims=True) + EPS) * gain


def _resident(shape):
    return pl.BlockSpec(shape, lambda *_: (0,) * len(shape), pipeline_mode=pl.Buffered(1))


def _resident_layer(stack_shape, layer):
    return pl.BlockSpec((None,) + tuple(stack_shape[1:]),
                        lambda *_: (layer,) + (0,) * (len(stack_shape) - 1),
                        pipeline_mode=pl.Buffered(1))


def _norm_proj_kernel(scales, x_ref, g_ref, w_ref, *out_refs):
    xn = _rms_norm(x_ref[...], g_ref[...]).astype(BF16)
    off = 0
    for o_ref, scale in zip(out_refs, scales):
        width = o_ref.shape[-1]
        y = jnp.dot(xn, w_ref[:, off:off + width], preferred_element_type=F32)
        o_ref[...] = (y if scale == 1.0 else y * scale).astype(o_ref.dtype)
        off += width


def norm_proj(x, gain, w_stack, layer, splits, row_tile=ROW_TILE):
    rows, d = x.shape
    n = w_stack.shape[2]
    assert sum(width for width, _, _ in splits) == n and rows % row_tile == 0
    return pl.pallas_call(
        functools.partial(_norm_proj_kernel, tuple(scale for _, _, scale in splits)),
        grid=(rows // row_tile,),
        in_specs=[pl.BlockSpec((row_tile, d), lambda i: (i, 0)),
                  _resident((1, d)), _resident_layer(w_stack.shape, layer)],
        out_specs=[pl.BlockSpec((row_tile, width), lambda i: (i, 0)) for width, _, _ in splits],
        out_shape=[jax.ShapeDtypeStruct((rows, width), dt) for width, dt, _ in splits],
        compiler_params=_params("parallel"),
        name="norm_proj",
    )(x, gain.reshape(1, d), w_stack)


def _xattn_update(h, gain, wq_ref, k_ref, v_ref, wo_ref):
    dh = h.shape[-1] // X_HEADS
    xn = _rms_norm(h, gain).astype(BF16)
    q = (jnp.dot(xn, wq_ref[...], preferred_element_type=F32) * (dh ** -0.5 * LOG2_E)).astype(BF16)
    outs = []
    for hd in range(X_HEADS):
        cols = slice(hd * dh, (hd + 1) * dh)
        s = lax.dot_general(q[:, cols], k_ref[0, :, cols], _NT, preferred_element_type=F32)
        p = jnp.exp2(s - jnp.max(s, axis=-1, keepdims=True))
        denom = jnp.sum(p, axis=-1, keepdims=True)
        o = jnp.dot(p.astype(BF16), v_ref[0, :, cols], preferred_element_type=F32)
        outs.append((o / denom).astype(BF16))
    return h + jnp.dot(jnp.concatenate(outs, axis=-1), wo_ref[...], preferred_element_type=F32)


def _mlp_update(h, gain, wu_ref, wd_ref, ff_chunk):
    xn = _rms_norm(h, gain).astype(BF16)
    acc = h
    for c in range(wu_ref.shape[1] // ff_chunk):
        cols = slice(c * ff_chunk, (c + 1) * ff_chunk)
        u = jnp.maximum(jnp.dot(xn, wu_ref[:, cols], preferred_element_type=F32), 0.0)
        acc = acc + jnp.dot((u * u).astype(BF16), wd_ref[cols, :], preferred_element_type=F32)
    return acc


def _layer_tail_kernel(n_act, final, ff_chunk, h_ref, *refs):
    a_refs = refs[:n_act]
    (wout_ref, gx_ref, wq_ref, k_ref, v_ref, wo_ref, gm_ref, wu_ref,
     wd_ref) = refs[n_act:n_act + 9]
    o_ref = refs[-1]
    h = h_ref[...]
    off = 0
    for a_ref in a_refs:
        width = a_ref.shape[1]
        h = h + jnp.dot(a_ref[...], wout_ref[off:off + width, :], preferred_element_type=F32)
        off += width
    h = _xattn_update(h, gx_ref[...], wq_ref, k_ref, v_ref, wo_ref)
    h = _mlp_update(h, gm_ref[...], wu_ref, wd_ref, ff_chunk)
    if final:
        h = _rms_norm(h, refs[-2][...])
    o_ref[...] = h


def layer_tail(h, acts, w_out, mixer, g_xattn, w_q, k_mem, v_mem, w_o, g_mlp, w_up, w_down, layer,
               seq, g_final=None, row_tile=ROW_TILE, ff_chunk=1024):
    rows, d = h.shape
    tiles_per_seq = seq // row_tile
    row_spec = lambda width: pl.BlockSpec((row_tile, width), lambda i: (i, 0))
    mem_spec = pl.BlockSpec((1,) + k_mem.shape[1:], lambda i: (i // tiles_per_seq, 0, 0))
    gains = [g.reshape(1, d) for g in (g_xattn, g_mlp)]
    operands = [h, *acts, w_out, gains[0], w_q, k_mem, v_mem, w_o, gains[1], w_up, w_down]
    in_specs = ([row_spec(d)] + [row_spec(a.shape[1]) for a in acts]
                + [_resident_layer(w_out.shape, mixer), _resident((1, d)),
                   _resident_layer(w_q.shape, layer), mem_spec, mem_spec,
                   _resident_layer(w_o.shape, layer), _resident((1, d)),
                   _resident_layer(w_up.shape, layer), _resident_layer(w_down.shape, layer)])
    if g_final is not None:
        operands.append(g_final.reshape(1, d))
        in_specs.append(_resident((1, d)))
    return pl.pallas_call(
        functools.partial(_layer_tail_kernel, len(acts), g_final is not None, ff_chunk),
        grid=(rows // row_tile,),
        in_specs=in_specs,
        out_specs=row_spec(d),
        out_shape=jax.ShapeDtypeStruct((rows, d), F32),
        compiler_params=pltpu.CompilerParams(dimension_semantics=("parallel",),
                                             vmem_limit_bytes=TAIL_VMEM_LIMIT_BYTES),
        name="layer_tail",
    )(*operands)


def _hgrn_tables():
    c = CHUNK
    t = np.arange(c)[:, None]
    j = np.arange(c)[None, :]
    blocks = [(j <= t), (j > t)]
    masks = [np.eye(c, dtype=bool)]
    for level in range(HGRN_LEVELS):
        size = c >> level
        half = size // 2
        ref = (t // size) * size + half - 1
        upper = (t % size) >= half
        blocks.append(np.where(upper, (j > ref) & (j <= t), (j > t) & (j <= ref)))
        tt, ss = np.arange(c)[:, None], np.arange(c)[None, :]
        masks.append((tt // size == ss // size) & ((tt % size) >= half) & ((ss % size) < half))
    w = np.concatenate(blocks, axis=0).astype(np.float32)
    return np.concatenate([w, w], axis=1), np.stack(masks).astype(np.float32)


def _hgrn_kernel(q_ref, f_ref, i_ref, g_ref, lb_ref, gain_ref, ww_ref, mask_ref, o_ref, st_ref):
    c, dk = CHUNK, A_HEAD_DIM

    @pl.when(pl.program_id(1) == 0)
    def _():
        st_ref[...] = jnp.zeros_like(st_ref)

    z = f_ref[...]
    lb = lb_ref[...]
    log_f = jnp.log2(lb + (1.0 - lb) * jax.nn.sigmoid(z))
    kk = (1.0 - lb) * jax.nn.sigmoid(-z)
    qf = q_ref[...].astype(F32)
    qs = qf * jax.nn.sigmoid(qf) * dk ** -0.5
    hi = log_f.astype(BF16)
    lo = (log_f - hi.astype(F32)).astype(BF16)
    row = lax.broadcasted_iota(jnp.int32, (c, q_ref.shape[1]), 0)

    head_cols = [slice(hd * dk, (hd + 1) * dk) for hd in range(A_HEADS)]
    n_chunks = q_ref.shape[0] // c

    independent = []
    for ci in range(n_chunks):
        rows = slice(ci * c, (ci + 1) * c)
        x = jnp.dot(ww_ref[...], jnp.concatenate([hi[rows], lo[rows]], axis=0),
                    preferred_element_type=F32)
        b = x[0:c]
        chunk_decay = jnp.exp2(b[c - 1:c, :])
        q_c, k_c, v = qs[rows], kk[rows], i_ref[rows, :]
        q_in = (q_c * jnp.exp2(b)).astype(BF16)
        k_out = (k_c * jnp.exp2(x[c:2 * c])).astype(BF16)
        sides = [(q_c.astype(BF16), k_c.astype(BF16))]
        for level in range(HGRN_LEVELS):
            half = (c >> level) // 2
            e = jnp.exp2(x[(2 + level) * c:(3 + level) * c])
            m = (jnp.where((row & half) != 0, q_c, k_c) * e).astype(BF16)
            sides.append((m, m))
        o_intra, upd_t = [], []
        for cols in head_cols:
            a = sum(lax.dot_general(lhs[:, cols], rhs[:, cols], _NT,
                                    preferred_element_type=F32) * mask_ref[n]
                    for n, (lhs, rhs) in enumerate(sides))
            o_intra.append(jnp.dot(a.astype(BF16), v[:, cols], preferred_element_type=F32))
            upd_t.append(lax.dot_general(v[:, cols], k_out[:, cols], _TN,
                                         preferred_element_type=F32))
        independent.append((q_in, chunk_decay, o_intra, upd_t))

    states = [st_ref[hd] for hd in range(A_HEADS)]
    for ci, (q_in, chunk_decay, o_intra, upd_t) in enumerate(independent):
        outs = []
        for hd, cols in enumerate(head_cols):
            o = o_intra[hd] + lax.dot_general(q_in[:, cols], states[hd].astype(BF16), _NT,
                                              preferred_element_type=F32)
            states[hd] = states[hd] * chunk_decay[:, cols] + upd_t[hd]
            outs.append(_rms_norm(o, gain_ref[:, cols]))
        rows = slice(ci * c, (ci + 1) * c)
        gf = g_ref[rows, :].astype(F32)
        o_ref[rows, :] = (jnp.concatenate(outs, axis=-1)
                          * (gf * jax.nn.sigmoid(gf))).astype(o_ref.dtype)
    for hd in range(A_HEADS):
        st_ref[hd] = states[hd]


HGRN_GROUP = 8


def hgrn2_mixer(q, f, i, g, lower_bound, out_gain, batch, seq):
    rows, width = q.shape
    n_chunks = seq // (HGRN_GROUP * CHUNK)
    ww, masks = _hgrn_tables()
    act = pl.BlockSpec((HGRN_GROUP * CHUNK, width), lambda b, n: (b * n_chunks + n, 0))
    return pl.pallas_call(
        _hgrn_kernel,
        grid=(batch, n_chunks),
        in_specs=[act, act, act, act, _resident((1, width)), _resident((1, width)),
                  _resident(ww.shape), _resident(masks.shape)],
        out_specs=act,
        out_shape=jax.ShapeDtypeStruct((rows, width), BF16),
        scratch_shapes=[pltpu.VMEM((A_HEADS, A_HEAD_DIM, A_HEAD_DIM), F32)],
        compiler_params=_params("parallel", "arbitrary"),
        name="hgrn2",
    )(q, f, i, g, lower_bound.reshape(1, width), out_gain.reshape(1, width),
      jnp.asarray(ww, BF16), jnp.asarray(masks))


BAND_GROUP = 4
BAND_ROWS = BAND_GROUP * CHUNK
BAND_WINDOW = (BAND_GROUP + B_PREV_CHUNKS) * CHUNK
BIAS_SPAN = BAND_ROWS + BAND_WINDOW


def _rel_bias_kernel(table_ref, o_ref):
    heads, n_idx = table_ref.shape
    e = lax.broadcasted_iota(jnp.int32, (n_idx, BIAS_SPAN), 1)
    distance = B_PREV_CHUNKS * CHUNK + BAND_ROWS - 1 - e
    idx = jnp.clip(distance, -REL_CLIP, REL_CLIP) + REL_CLIP
    onehot = (lax.broadcasted_iota(jnp.int32, (n_idx, BIAS_SPAN), 0) == idx).astype(F32)
    by_offset = jnp.dot(table_ref[...], onehot, preferred_element_type=F32,
                        precision=lax.Precision.HIGHEST) * LOG2_E
    col = lax.broadcasted_iota(jnp.int32, (heads, BAND_WINDOW), 1)
    for r in range(BAND_ROWS):
        band_start = (r // CHUNK) * CHUNK
        in_band = (col >= band_start) & (col < band_start + B_BAND * CHUNK)
        shift = BAND_ROWS - 1 - r
        o_ref[:, r * BAND_WINDOW:(r + 1) * BAND_WINDOW] = jnp.where(
            in_band, by_offset[:, shift:shift + BAND_WINDOW], -jnp.inf)


def rel_bias(rel_table):
    heads, n_rel = rel_table.shape
    n_idx = -(-n_rel // LANES) * LANES
    table = jnp.pad(rel_table.astype(F32), ((0, 0), (0, n_idx - n_rel)))
    out = pl.pallas_call(
        _rel_bias_kernel,
        in_specs=[pl.BlockSpec(memory_space=pltpu.VMEM)],
        out_specs=pl.BlockSpec(memory_space=pltpu.VMEM),
        out_shape=jax.ShapeDtypeStruct((heads, BAND_ROWS * BAND_WINDOW), F32),
        compiler_params=pltpu.CompilerParams(vmem_limit_bytes=VMEM_LIMIT_BYTES),
        name="rel_bias",
    )(table)
    return out.reshape(heads, BAND_ROWS, BAND_WINDOW)


def _band_attn_kernel(q_ref, k_ref, v_ref, bias_ref, o_ref):
    g = pl.program_id(1)
    c, dh = CHUNK, B_HEAD_DIM
    n_slots = BAND_WINDOW // c
    first = g * BAND_GROUP - B_PREV_CHUNKS
    lane = lax.broadcasted_iota(jnp.int32, (BAND_ROWS, 2 * dh), 1)
    lane_kv = lax.broadcasted_iota(jnp.int32, (BAND_WINDOW, 2 * dh), 1)

    def attend(at_sequence_start):
        starts = [pl.multiple_of(jnp.maximum(first + j, 0) * c, c) for j in range(n_slots)]
        k_win = jnp.concatenate([k_ref[0, pl.ds(s, c), :] for s in starts], axis=0)
        v_win = jnp.concatenate([v_ref[0, pl.ds(s, c), :] for s in starts], axis=0)
        started = lax.broadcasted_iota(jnp.int32, (BAND_ROWS, BAND_WINDOW), 1) >= -first * c
        q_all = q_ref[...]
        for pair in range(B_HEADS // 2):
            cols = slice(pair * 2 * dh, (pair + 1) * 2 * dh)
            q_pair, k_pair, v_pair = q_all[:, cols], k_win[:, cols], v_win[:, cols]
            q_both = jnp.concatenate([jnp.where(lane < dh, q_pair, jnp.zeros_like(q_pair)),
                                      jnp.where(lane >= dh, q_pair, jnp.zeros_like(q_pair))],
                                     axis=0)
            s_both = lax.dot_general(q_both, k_pair, _NT, preferred_element_type=F32)
            outs = []
            for sub in range(2):
                s = s_both[sub * BAND_ROWS:(sub + 1) * BAND_ROWS] + bias_ref[2 * pair + sub]
                if at_sequence_start:
                    s = jnp.where(started, s, -jnp.inf)
                p = jnp.exp2(s - jnp.max(s, axis=-1, keepdims=True)).astype(BF16)
                ones_lane = dh if sub == 0 else 0
                own_kv = (lane_kv < dh) if sub == 0 else (lane_kv >= dh)
                v_h = jnp.where(own_kv, v_pair,
                                jnp.where(lane_kv == ones_lane, 1.0, 0.0).astype(v_pair.dtype))
                o = jnp.dot(p, v_h, preferred_element_type=F32)
                outs.append(o / o[:, ones_lane:ones_lane + 1])
            o_ref[:, cols] = jnp.where(lane < dh, outs[0], outs[1]).astype(o_ref.dtype)

    pl.when(first < 0)(functools.partial(attend, True))
    pl.when(first >= 0)(functools.partial(attend, False))


def band_attention(q, k, v, bias, batch, seq):
    rows, width = q.shape
    n_groups = seq // BAND_ROWS
    kv_spec = pl.BlockSpec((1, seq, width), lambda b, g: (b, 0, 0))
    return pl.pallas_call(
        _band_attn_kernel,
        grid=(batch, n_groups),
        in_specs=[pl.BlockSpec((BAND_ROWS, width), lambda b, g: (b * n_groups + g, 0)),
                  kv_spec, kv_spec, _resident(bias.shape)],
        out_specs=pl.BlockSpec((BAND_ROWS, width), lambda b, g: (b * n_groups + g, 0)),
        out_shape=jax.ShapeDtypeStruct((rows, width), BF16),
        compiler_params=_params("parallel", "arbitrary"),
        name="band_attn",
    )(q, k.reshape(batch, seq, width), v.reshape(batch, seq, width), bias)


def _fox_cum_kernel(blk, f_ref, bias_ref, tril_ref, o_ref):
    seq = f_ref.shape[1]
    carry = jnp.zeros((1, f_ref.shape[2]), F32)
    for c in range(seq // blk):
        x = f_ref[0, c * blk:(c + 1) * blk, :] + bias_ref[...]
        log_sig = jnp.minimum(x, 0.0) - jnp.log1p(jnp.exp(-jnp.abs(x)))
        cs = jnp.dot(tril_ref[...], log_sig, preferred_element_type=F32,
                     precision=lax.Precision.HIGHEST) + carry
        carry = cs[blk - 1:blk, :]
        o_ref[0, :, c * blk:(c + 1) * blk] = (cs * LOG2_E).T[:o_ref.shape[1]]


def fox_cum(f, f_bias, batch, seq, blk=256):
    width = f.shape[-1]
    n_gates = f_bias.shape[0]
    bias = jnp.pad(f_bias.astype(F32), (0, width - n_gates)).reshape(1, width)
    tril = jnp.asarray(np.tril(np.ones((blk, blk), np.float32)))
    return pl.pallas_call(
        functools.partial(_fox_cum_kernel, blk),
        grid=(batch,),
        in_specs=[pl.BlockSpec((1, seq, width), lambda b: (b, 0, 0)), _resident((1, width)),
                  _resident((blk, blk))],
        out_specs=pl.BlockSpec((1, n_gates, seq), lambda b: (b, 0, 0)),
        out_shape=jax.ShapeDtypeStruct((batch, n_gates, seq), F32),
        compiler_params=_params("parallel"),
        name="fox_cum",
    )(f.reshape(batch, seq, width), bias, tril)


def _fox_attn_kernel(blk, q_ref, k_ref, v_ref, cum_ref, o_ref):
    step = pl.program_id(2)
    dh = C_HEAD_DIM
    n_blk = q_ref.shape[1] // blk
    lane = lax.broadcasted_iota(jnp.int32, (blk, 2 * dh), 1)
    key_le_query = (lax.broadcasted_iota(jnp.int32, (blk, blk), 0)
                    <= lax.broadcasted_iota(jnp.int32, (blk, blk), 1))
    row_o = lax.broadcasted_iota(jnp.int32, (2 * dh, blk), 0)

    def attend(n_past):
        past = n_past * blk
        row_v = lax.broadcasted_iota(jnp.int32, (2 * dh, past + blk), 0)
        cum_cols = cum_ref[0, 0, :, :past + blk].T
        for pair in range(q_ref.shape[2] // (2 * dh)):
            cols = slice(pair * 2 * dh, (pair + 1) * 2 * dh)
            q_pair = q_ref[0, past:past + blk, cols]
            k_all, v_all = k_ref[0, :past + blk, cols], v_ref[0, :past + blk, cols]
            q_both = jnp.concatenate([jnp.where(lane < dh, q_pair, jnp.zeros_like(q_pair)),
                                      jnp.where(lane >= dh, q_pair, jnp.zeros_like(q_pair))],
                                     axis=0)
            st_both = lax.dot_general(k_all, q_both, _NT, preferred_element_type=F32)
            v_t = v_all.T
            outs = []
            for sub in range(2):
                gate = 2 * pair + sub
                st = st_both[:, sub * blk:(sub + 1) * blk] - cum_cols[:, gate:gate + 1]
                parts = [jnp.where(key_le_query, st[past:], -jnp.inf)]
                if n_past:
                    parts.insert(0, st[:past])
                m = functools.reduce(jnp.maximum,
                                     [jnp.max(t, axis=0, keepdims=True) for t in parts])
                p_t = jnp.concatenate([jnp.exp2(t - m).astype(BF16) for t in parts], axis=0)
                ones_row = dh if sub == 0 else 0
                own = (row_v < dh) if sub == 0 else (row_v >= dh)
                v_h_t = jnp.where(own, v_t,
                                  jnp.where(row_v == ones_row, 1.0, 0.0).astype(v_t.dtype))
                o_t = jnp.dot(v_h_t, p_t, preferred_element_type=F32)
                outs.append(o_t / o_t[ones_row:ones_row + 1, :])
            o_pair_t = jnp.where(row_o < dh, outs[0], outs[1])
            o_ref[0, past:past + blk, cols] = o_pair_t.T.astype(o_ref.dtype)

    def attend_both(i):
        attend(i)
        attend(n_blk - 1 - i)

    for i in range(n_blk // 2):
        pl.when(step == i)(functools.partial(attend_both, i))


FOX_PAIRS_PER_STEP = 4


def fox_attention(q, k, v, cum_t, batch, seq, blk=FOX_BLOCK):
    width = q.shape[-1]
    pairs = C_HEADS // 2 // FOX_PAIRS_PER_STEP
    pair_w = 2 * C_HEAD_DIM * FOX_PAIRS_PER_STEP
    gates = 2 * FOX_PAIRS_PER_STEP
    cum = jnp.pad(cum_t.reshape(batch, C_HEADS // gates, gates, seq),
                  ((0, 0), (0, 0), (0, SUBLANES - gates), (0, 0)))
    q3, k3, v3 = (t.reshape(batch, seq, width) for t in (q, k, v))
    seq_spec = pl.BlockSpec((1, seq, pair_w), lambda b, p, i: (b, 0, p))
    assert (seq // blk) % 2 == 0
    out = pl.pallas_call(
        functools.partial(_fox_attn_kernel, blk),
        grid=(batch, pairs, seq // blk // 2),
        in_specs=[seq_spec, seq_spec, seq_spec,
                  pl.BlockSpec((1, 1, SUBLANES, seq), lambda b, p, i: (b, p, 0, 0))],
        out_specs=seq_spec,
        out_shape=jax.ShapeDtypeStruct((batch, seq, width), BF16),
        compiler_params=_params("parallel", "parallel", "arbitrary"),
        name="fox_attn",
    )(q3, k3, v3, cum)
    return out.reshape(batch * seq, width)


def _even_mixer(h, gain, w_in_stack, e, lower_bound, out_gain, rel_table, batch, seq):
    a_w, b_w = A_HEADS * A_HEAD_DIM, B_HEADS * B_HEAD_DIM
    splits = [(a_w, BF16, 1.0), (a_w, F32, 1.0), (a_w, BF16, 1.0), (a_w, BF16, 1.0),
              (b_w, BF16, B_HEAD_DIM ** -0.5 * LOG2_E), (b_w, BF16, 1.0), (b_w, BF16, 1.0)]
    q_a, f_a, i_a, g_a, q_b, k_b, v_b = norm_proj(h, gain, w_in_stack, e, splits)
    o_a = hgrn2_mixer(q_a, f_a, i_a, g_a, lower_bound, out_gain, batch, seq)
    o_b = band_attention(q_b, k_b, v_b, rel_bias(rel_table), batch, seq)
    return [o_a, o_b]


def _pad_gate_columns(w_in_c):
    return jnp.pad(w_in_c, ((0, 0), (0, 0), (0, LANES - C_HEADS)))


def _odd_mixer(h, gain, w_in_stack, o, f_bias, batch, seq):
    c_w = C_HEADS * C_HEAD_DIM
    splits = [(c_w, BF16, C_HEAD_DIM ** -0.5 * LOG2_E), (c_w, BF16, 1.0), (c_w, BF16, 1.0),
              (LANES, F32, 1.0)]
    q, k, v, f = norm_proj(h, gain, w_in_stack, o, splits)
    cum_t = fox_cum(f, f_bias, batch, seq)
    return [fox_attention(q, k, v, cum_t, batch, seq)]


def kernel(x, mem, norm_mix, norm_xattn, norm_mem, norm_mlp, norm_final, w_in_ab, a_lb_logits,
           a_out_gain, b_rel_bias, w_out_ab, w_in_c, c_fgate_bias, w_out_c, w_xq, w_xkv, w_xo,
           w_up, w_down):
    batch, seq, d = x.shape
    mem_tokens = mem.shape[1]
    depth = norm_mix.shape[0]
    lb_w = jax.nn.softmax(a_lb_logits.astype(F32), axis=0)
    lower_bounds = jnp.cumsum(lb_w, axis=0) - lb_w[0]
    h = x.reshape(batch * seq, d)
    mem2 = mem.reshape(batch * mem_tokens, d)
    w_in_ab, w_in_c = w_in_ab.astype(BF16), _pad_gate_columns(w_in_c).astype(BF16)
    w_out_ab, w_out_c = w_out_ab.astype(BF16), w_out_c.astype(BF16)
    w_xq, w_xkv, w_xo = w_xq.astype(BF16), w_xkv.astype(BF16), w_xo.astype(BF16)
    w_up, w_down = w_up.astype(BF16), w_down.astype(BF16)
    for layer in range(depth):
        mixer = layer // 2
        if layer % 2 == 0:
            acts = _even_mixer(h, norm_mix[layer], w_in_ab, mixer, lower_bounds[mixer],
                               a_out_gain[mixer], b_rel_bias[mixer], batch, seq)
            w_out = w_out_ab
        else:
            acts = _odd_mixer(h, norm_mix[layer], w_in_c, mixer, c_fgate_bias[mixer], batch, seq)
            w_out = w_out_c
        k_mem, v_mem = norm_proj(mem2, norm_mem[layer], w_xkv, layer,
                                 [(d, BF16, 1.0), (d, BF16, 1.0)])
        h = layer_tail(h, acts, w_out, mixer, norm_xattn[layer], w_xq,
                       k_mem.reshape(batch, mem_tokens, d), v_mem.reshape(batch, mem_tokens, d),
                       w_xo, norm_mlp[layer], w_up, w_down, layer, seq,
                       g_final=norm_final if layer == depth - 1 else None)
    return h.reshape(batch, seq, d)
```

```python
import functools

import numpy as np
import jax
import jax.numpy as jnp
from jax import lax
from jax.experimental import pallas as pl
from jax.experimental.pallas import tpu as pltpu

F32 = jnp.float32
BF16 = jnp.bfloat16

EPS = 1e-6
LOG2_E = 1.4426950408889634
CHUNK = 64
A_HEADS = 4
A_HEAD_DIM = 128
B_HEADS = 8
B_HEAD_DIM = 64
B_PREV_CHUNKS = 8
B_BAND = B_PREV_CHUNKS + 1
REL_CLIP = 2 * CHUNK
C_HEADS = 16
C_HEAD_DIM = 64
X_HEADS = 4
HGRN_LEVELS = 6

LANES = 128
SUBLANES = 8
VMEM_LIMIT_BYTES = 48 * 1024 * 1024
TAIL_VMEM_LIMIT_BYTES = 56 * 1024 * 1024

ROW_TILE = 512
FOX_BLOCK = 512

_NT = (((1,), (1,)), ((), ()))
_TN = (((0,), (0,)), ((), ()))


def _params(*semantics):
    return pltpu.CompilerParams(dimension_semantics=semantics, vmem_limit_bytes=VMEM_LIMIT_BYTES)


def _rms_norm(x, gain):
    return x * lax.rsqrt(jnp.mean(x * x, axis=-1, keepdims=True) + EPS) * gain


def _resident(shape):
    return pl.BlockSpec(shape, lambda *_: (0,) * len(shape), pipeline_mode=pl.Buffered(1))


def _resident_layer(stack_shape, layer):
    return pl.BlockSpec((None,) + tuple(stack_shape[1:]),
                        lambda *_: (layer,) + (0,) * (len(stack_shape) - 1),
                        pipeline_mode=pl.Buffered(1))


def _norm_proj_kernel(scales, x_ref, g_ref, w_ref, *out_refs):
    xn = _rms_norm(x_ref[...], g_ref[...]).astype(BF16)
    off = 0
    for o_ref, scale in zip(out_refs, scales):
        width = o_ref.shape[-1]
        y = jnp.dot(xn, w_ref[:, off:off + width], preferred_element_type=F32)
        o_ref[...] = (y if scale == 1.0 else y * scale).astype(o_ref.dtype)
        off += width


def norm_proj(x, gain, w_stack, layer, splits, row_tile=ROW_TILE):
    rows, d = x.shape
    n = w_stack.shape[2]
    assert sum(width for width, _, _ in splits) == n and rows % row_tile == 0
    return pl.pallas_call(
        functools.partial(_norm_proj_kernel, tuple(scale for _, _, scale in splits)),
        grid=(rows // row_tile,),
        in_specs=[pl.BlockSpec((row_tile, d), lambda i: (i, 0)),
                  _resident((1, d)), _resident_layer(w_stack.shape, layer)],
        out_specs=[pl.BlockSpec((row_tile, width), lambda i: (i, 0)) for width, _, _ in splits],
        out_shape=[jax.ShapeDtypeStruct((rows, width), dt) for width, dt, _ in splits],
        compiler_params=_params("parallel"),
        name="norm_proj",
    )(x, gain.reshape(1, d), w_stack)


def _xattn_update(h, gain, wq_ref, k_ref, v_ref, wo_ref):
    dh = h.shape[-1] // X_HEADS
    xn = _rms_norm(h, gain).astype(BF16)
    q = (jnp.dot(xn, wq_ref[...], preferred_element_type=F32) * (dh ** -0.5 * LOG2_E)).astype(BF16)
    outs = []
    for hd in range(X_HEADS):
        cols = slice(hd * dh, (hd + 1) * dh)
        s = lax.dot_general(q[:, cols], k_ref[0, :, cols], _NT, preferred_element_type=F32)
        p = jnp.exp2(s - jnp.max(s, axis=-1, keepdims=True))
        denom = jnp.sum(p, axis=-1, keepdims=True)
        o = jnp.dot(p.astype(BF16), v_ref[0, :, cols], preferred_element_type=F32)
        outs.append((o / denom).astype(BF16))
    return h + jnp.dot(jnp.concatenate(outs, axis=-1), wo_ref[...], preferred_element_type=F32)


def _mlp_update(h, gain, wu_ref, wd_ref, ff_chunk):
    xn = _rms_norm(h, gain).astype(BF16)
    acc = h
    for c in range(wu_ref.shape[1] // ff_chunk):
        cols = slice(c * ff_chunk, (c + 1) * ff_chunk)
        u = jnp.maximum(jnp.dot(xn, wu_ref[:, cols], preferred_element_type=F32), 0.0)
        acc = acc + jnp.dot((u * u).astype(BF16), wd_ref[cols, :], preferred_element_type=F32)
    return acc


def _layer_tail_kernel(n_act, final, ff_chunk, h_ref, *refs):
    a_refs = refs[:n_act]
    (wout_ref, gx_ref, wq_ref, k_ref, v_ref, wo_ref, gm_ref, wu_ref,
     wd_ref) = refs[n_act:n_act + 9]
    o_ref = refs[-1]
    h = h_ref[...]
    off = 0
    for a_ref in a_refs:
        width = a_ref.shape[1]
        h = h + jnp.dot(a_ref[...], wout_ref[off:off + width, :], preferred_element_type=F32)
        off += width
    h = _xattn_update(h, gx_ref[...], wq_ref, k_ref, v_ref, wo_ref)
    h = _mlp_update(h, gm_ref[...], wu_ref, wd_ref, ff_chunk)
    if final:
        h = _rms_norm(h, refs[-2][...])
    o_ref[...] = h


def layer_tail(h, acts, w_out, mixer, g_xattn, w_q, k_mem, v_mem, w_o, g_mlp, w_up, w_down, layer,
               seq, g_final=None, row_tile=ROW_TILE, ff_chunk=1024):
    rows, d = h.shape
    tiles_per_seq = seq // row_tile
    row_spec = lambda width: pl.BlockSpec((row_tile, width), lambda i: (i, 0))
    mem_spec = pl.BlockSpec((1,) + k_mem.shape[1:], lambda i: (i // tiles_per_seq, 0, 0))
    gains = [g.reshape(1, d) for g in (g_xattn, g_mlp)]
    operands = [h, *acts, w_out, gains[0], w_q, k_mem, v_mem, w_o, gains[1], w_up, w_down]
    in_specs = ([row_spec(d)] + [row_spec(a.shape[1]) for a in acts]
                + [_resident_layer(w_out.shape, mixer), _resident((1, d)),
                   _resident_layer(w_q.shape, layer), mem_spec, mem_spec,
                   _resident_layer(w_o.shape, layer), _resident((1, d)),
                   _resident_layer(w_up.shape, layer), _resident_layer(w_down.shape, layer)])
    if g_final is not None:
        operands.append(g_final.reshape(1, d))
        in_specs.append(_resident((1, d)))
    return pl.pallas_call(
        functools.partial(_layer_tail_kernel, len(acts), g_final is not None, ff_chunk),
        grid=(rows // row_tile,),
        in_specs=in_specs,
        out_specs=row_spec(d),
        out_shape=jax.ShapeDtypeStruct((rows, d), F32),
        compiler_params=pltpu.CompilerParams(dimension_semantics=("parallel",),
                                             vmem_limit_bytes=TAIL_VMEM_LIMIT_BYTES),
        name="layer_tail",
    )(*operands)


def _hgrn_tables():
    c = CHUNK
    t = np.arange(c)[:, None]
    j = np.arange(c)[None, :]
    blocks = [(j <= t), (j > t)]
    masks = [np.eye(c, dtype=bool)]
    for level in range(HGRN_LEVELS):
        size = c >> level
        half = size // 2
        ref = (t // size) * size + half - 1
        upper = (t % size) >= half
        blocks.append(np.where(upper, (j > ref) & (j <= t), (j > t) & (j <= ref)))
        tt, ss = np.arange(c)[:, None], np.arange(c)[None, :]
        masks.append((tt // size == ss // size) & ((tt % size) >= half) & ((ss % size) < half))
    w = np.concatenate(blocks, axis=0).astype(np.float32)
    return np.concatenate([w, w], axis=1), np.stack(masks).astype(np.float32)


def _hgrn_kernel(q_ref, f_ref, i_ref, g_ref, lb_ref, gain_ref, ww_ref, mask_ref, o_ref, st_ref):
    c, dk = CHUNK, A_HEAD_DIM

    @pl.when(pl.program_id(1) == 0)
    def _():
        st_ref[...] = jnp.zeros_like(st_ref)

    z = f_ref[...]
    lb = lb_ref[...]
    log_f = jnp.log2(lb + (1.0 - lb) * jax.nn.sigmoid(z))
    kk = (1.0 - lb) * jax.nn.sigmoid(-z)
    qf = q_ref[...].astype(F32)
    qs = qf * jax.nn.sigmoid(qf) * dk ** -0.5
    hi = log_f.astype(BF16)
    lo = (log_f - hi.astype(F32)).astype(BF16)
    row = lax.broadcasted_iota(jnp.int32, (c, q_ref.shape[1]), 0)

    head_cols = [slice(hd * dk, (hd + 1) * dk) for hd in range(A_HEADS)]
    n_chunks = q_ref.shape[0] // c

    independent = []
    for ci in range(n_chunks):
        rows = slice(ci * c, (ci + 1) * c)
        x = jnp.dot(ww_ref[...], jnp.concatenate([hi[rows], lo[rows]], axis=0),
                    preferred_element_type=F32)
        b = x[0:c]
        chunk_decay = jnp.exp2(b[c - 1:c, :])
        q_c, k_c, v = qs[rows], kk[rows], i_ref[rows, :]
        q_in = (q_c * jnp.exp2(b)).astype(BF16)
        k_out = (k_c * jnp.exp2(x[c:2 * c])).astype(BF16)
        sides = [(q_c.astype(BF16), k_c.astype(BF16))]
        for level in range(HGRN_LEVELS):
            half = (c >> level) // 2
            e = jnp.exp2(x[(2 + level) * c:(3 + level) * c])
            m = (jnp.where((row & half) != 0, q_c, k_c) * e).astype(BF16)
            sides.append((m, m))
        o_intra, upd_t = [], []
        for cols in head_cols:
            a = sum(lax.dot_general(lhs[:, cols], rhs[:, cols], _NT,
                                    preferred_element_type=F32) * mask_ref[n]
                    for n, (lhs, rhs) in enumerate(sides))
            o_intra.append(jnp.dot(a.astype(BF16), v[:, cols], preferred_element_type=F32))
            upd_t.append(lax.dot_general(v[:, cols], k_out[:, cols], _TN,
                                         preferred_element_type=F32))
        independent.append((q_in, chunk_decay, o_intra, upd_t))

    states = [st_ref[hd] for hd in range(A_HEADS)]
    for ci, (q_in, chunk_decay, o_intra, upd_t) in enumerate(independent):
        outs = []
        for hd, cols in enumerate(head_cols):
            o = o_intra[hd] + lax.dot_general(q_in[:, cols], states[hd].astype(BF16), _NT,
                                              preferred_element_type=F32)
            states[hd] = states[hd] * chunk_decay[:, cols] + upd_t[hd]
            outs.append(_rms_norm(o, gain_ref[:, cols]))
        rows = slice(ci * c, (ci + 1) * c)
        gf = g_ref[rows, :].astype(F32)
        o_ref[rows, :] = (jnp.concatenate(outs, axis=-1)
                          * (gf * jax.nn.sigmoid(gf))).astype(o_ref.dtype)
    for hd in range(A_HEADS):
        st_ref[hd] = states[hd]


HGRN_GROUP = 8


def hgrn2_mixer(q, f, i, g, lower_bound, out_gain, batch, seq):
    rows, width = q.shape
    n_chunks = seq // (HGRN_GROUP * CHUNK)
    ww, masks = _hgrn_tables()
    act = pl.BlockSpec((HGRN_GROUP * CHUNK, width), lambda b, n: (b * n_chunks + n, 0))
    return pl.pallas_call(
        _hgrn_kernel,
        grid=(batch, n_chunks),
        in_specs=[act, act, act, act, _resident((1, width)), _resident((1, width)),
                  _resident(ww.shape), _resident(masks.shape)],
        out_specs=act,
        out_shape=jax.ShapeDtypeStruct((rows, width), BF16),
        scratch_shapes=[pltpu.VMEM((A_HEADS, A_HEAD_DIM, A_HEAD_DIM), F32)],
        compiler_params=_params("parallel", "arbitrary"),
        name="hgrn2",
    )(q, f, i, g, lower_bound.reshape(1, width), out_gain.reshape(1, width),
      jnp.asarray(ww, BF16), jnp.asarray(masks))


BAND_GROUP = 4
BAND_ROWS = BAND_GROUP * CHUNK
BAND_WINDOW = (BAND_GROUP + B_PREV_CHUNKS) * CHUNK
BIAS_SPAN = BAND_ROWS + BAND_WINDOW


def _rel_bias_kernel(table_ref, o_ref):
    heads, n_idx = table_ref.shape
    e = lax.broadcasted_iota(jnp.int32, (n_idx, BIAS_SPAN), 1)
    distance = B_PREV_CHUNKS * CHUNK + BAND_ROWS - 1 - e
    idx = jnp.clip(distance, -REL_CLIP, REL_CLIP) + REL_CLIP
    onehot = (lax.broadcasted_iota(jnp.int32, (n_idx, BIAS_SPAN), 0) == idx).astype(F32)
    by_offset = jnp.dot(table_ref[...], onehot, preferred_element_type=F32,
                        precision=lax.Precision.HIGHEST) * LOG2_E
    col = lax.broadcasted_iota(jnp.int32, (heads, BAND_WINDOW), 1)
    for r in range(BAND_ROWS):
        band_start = (r // CHUNK) * CHUNK
        in_band = (col >= band_start) & (col < band_start + B_BAND * CHUNK)
        shift = BAND_ROWS - 1 - r
        o_ref[:, r * BAND_WINDOW:(r + 1) * BAND_WINDOW] = jnp.where(
            in_band, by_offset[:, shift:shift + BAND_WINDOW], -jnp.inf)


def rel_bias(rel_table):
    heads, n_rel = rel_table.shape
    n_idx = -(-n_rel // LANES) * LANES
    table = jnp.pad(rel_table.astype(F32), ((0, 0), (0, n_idx - n_rel)))
    out = pl.pallas_call(
        _rel_bias_kernel,
        in_specs=[pl.BlockSpec(memory_space=pltpu.VMEM)],
        out_specs=pl.BlockSpec(memory_space=pltpu.VMEM),
        out_shape=jax.ShapeDtypeStruct((heads, BAND_ROWS * BAND_WINDOW), F32),
        compiler_params=pltpu.CompilerParams(vmem_limit_bytes=VMEM_LIMIT_BYTES),
        name="rel_bias",
    )(table)
    return out.reshape(heads, BAND_ROWS, BAND_WINDOW)


def _band_attn_kernel(q_ref, k_ref, v_ref, bias_ref, o_ref):
    c, dh = CHUNK, B_HEAD_DIM
    n_slots = BAND_WINDOW // c
    lane = lax.broadcasted_iota(jnp.int32, (BAND_ROWS, 2 * dh), 1)
    lane_kv = lax.broadcasted_iota(jnp.int32, (BAND_WINDOW, 2 * dh), 1)

    def attend(at_sequence_start, local):
        group = pl.program_id(1) * BAND_STEP_GROUPS + local
        first = group * BAND_GROUP - B_PREV_CHUNKS
        rows = slice(local * BAND_ROWS, (local + 1) * BAND_ROWS)
        starts = [pl.multiple_of(jnp.maximum(first + j, 0) * c, c) for j in range(n_slots)]
        k_win = jnp.concatenate([k_ref[0, pl.ds(s, c), :] for s in starts], axis=0)
        v_win = jnp.concatenate([v_ref[0, pl.ds(s, c), :] for s in starts], axis=0)
        started = lax.broadcasted_iota(jnp.int32, (BAND_ROWS, BAND_WINDOW), 1) >= -first * c
        q_all = q_ref[rows, :]
        for pair in range(B_HEADS // 2):
            cols = slice(pair * 2 * dh, (pair + 1) * 2 * dh)
            q_pair, k_pair, v_pair = q_all[:, cols], k_win[:, cols], v_win[:, cols]
            q_both = jnp.concatenate([jnp.where(lane < dh, q_pair, jnp.zeros_like(q_pair)),
                                      jnp.where(lane >= dh, q_pair, jnp.zeros_like(q_pair))],
                                     axis=0)
            s_both = lax.dot_general(q_both, k_pair, _NT, preferred_element_type=F32)
            outs = []
            for sub in range(2):
                s = s_both[sub * BAND_ROWS:(sub + 1) * BAND_ROWS] + bias_ref[2 * pair + sub]
                if at_sequence_start:
                    s = jnp.where(started, s, -jnp.inf)
                p = jnp.exp2(s - jnp.max(s, axis=-1, keepdims=True)).astype(BF16)
                ones_lane = dh if sub == 0 else 0
                own_kv = (lane_kv < dh) if sub == 0 else (lane_kv >= dh)
                v_h = jnp.where(own_kv, v_pair,
                                jnp.where(lane_kv == ones_lane, 1.0, 0.0).astype(v_pair.dtype))
                o = jnp.dot(p, v_h, preferred_element_type=F32)
                outs.append(o / o[:, ones_lane:ones_lane + 1])
            o_ref[rows, cols] = jnp.where(lane < dh, outs[0], outs[1]).astype(o_ref.dtype)

    def attend_all(at_sequence_start):
        for local in range(BAND_STEP_GROUPS):
            attend(at_sequence_start, local)

    pl.when(pl.program_id(1) == 0)(functools.partial(attend_all, True))
    pl.when(pl.program_id(1) > 0)(functools.partial(attend_all, False))


BAND_STEP_GROUPS = B_PREV_CHUNKS // BAND_GROUP


def band_attention(q, k, v, bias, batch, seq):
    rows, width = q.shape
    step_rows = BAND_STEP_GROUPS * BAND_ROWS
    assert BAND_STEP_GROUPS * BAND_GROUP == B_PREV_CHUNKS and seq % step_rows == 0
    n_groups = seq // step_rows
    kv_spec = pl.BlockSpec((1, seq, width), lambda b, g: (b, 0, 0))
    return pl.pallas_call(
        _band_attn_kernel,
        grid=(batch, n_groups),
        in_specs=[pl.BlockSpec((step_rows, width), lambda b, g: (b * n_groups + g, 0)),
                  kv_spec, kv_spec, _resident(bias.shape)],
        out_specs=pl.BlockSpec((step_rows, width), lambda b, g: (b * n_groups + g, 0)),
        out_shape=jax.ShapeDtypeStruct((rows, width), BF16),
        compiler_params=_params("parallel", "arbitrary"),
        name="band_attn",
    )(q, k.reshape(batch, seq, width), v.reshape(batch, seq, width), bias)


def _fox_cum_kernel(blk, f_ref, bias_ref, tril_ref, o_ref):
    seq = f_ref.shape[1]
    carry = jnp.zeros((1, f_ref.shape[2]), F32)
    for c in range(seq // blk):
        x = f_ref[0, c * blk:(c + 1) * blk, :] + bias_ref[...]
        log_sig = jnp.minimum(x, 0.0) - jnp.log1p(jnp.exp(-jnp.abs(x)))
        cs = jnp.dot(tril_ref[...], log_sig, preferred_element_type=F32,
                     precision=lax.Precision.HIGHEST) + carry
        carry = cs[blk - 1:blk, :]
        o_ref[0, :, c * blk:(c + 1) * blk] = (cs * LOG2_E).T[:o_ref.shape[1]]


def fox_cum(f, f_bias, batch, seq, blk=256):
    width = f.shape[-1]
    n_gates = f_bias.shape[0]
    bias = jnp.pad(f_bias.astype(F32), (0, width - n_gates)).reshape(1, width)
    tril = jnp.asarray(np.tril(np.ones((blk, blk), np.float32)))
    return pl.pallas_call(
        functools.partial(_fox_cum_kernel, blk),
        grid=(batch,),
        in_specs=[pl.BlockSpec((1, seq, width), lambda b: (b, 0, 0)), _resident((1, width)),
                  _resident((blk, blk))],
        out_specs=pl.BlockSpec((1, n_gates, seq), lambda b: (b, 0, 0)),
        out_shape=jax.ShapeDtypeStruct((batch, n_gates, seq), F32),
        compiler_params=_params("parallel"),
        name="fox_cum",
    )(f.reshape(batch, seq, width), bias, tril)


def _fox_attn_kernel(blk, q_ref, k_ref, v_ref, cum_ref, o_ref):
    step = pl.program_id(2)
    dh = C_HEAD_DIM
    n_blk = q_ref.shape[1] // blk
    lane = lax.broadcasted_iota(jnp.int32, (blk, 2 * dh), 1)
    key_le_query = (lax.broadcasted_iota(jnp.int32, (blk, blk), 0)
                    <= lax.broadcasted_iota(jnp.int32, (blk, blk), 1))
    row_o = lax.broadcasted_iota(jnp.int32, (2 * dh, blk), 0)

    def attend(n_past):
        past = n_past * blk
        row_v = lax.broadcasted_iota(jnp.int32, (2 * dh, past + blk), 0)
        cum_cols = cum_ref[0, 0, :, :past + blk].T
        for pair in range(q_ref.shape[2] // (2 * dh)):
            cols = slice(pair * 2 * dh, (pair + 1) * 2 * dh)
            q_pair = q_ref[0, past:past + blk, cols]
            k_all, v_all = k_ref[0, :past + blk, cols], v_ref[0, :past + blk, cols]
            q_both = jnp.concatenate([jnp.where(lane < dh, q_pair, jnp.zeros_like(q_pair)),
                                      jnp.where(lane >= dh, q_pair, jnp.zeros_like(q_pair))],
                                     axis=0)
            st_both = lax.dot_general(k_all, q_both, _NT, preferred_element_type=F32)
            v_t = v_all.T
            outs = []
            for sub in range(2):
                gate = 2 * pair + sub
                st = st_both[:, sub * blk:(sub + 1) * blk] - cum_cols[:, gate:gate + 1]
                parts = [jnp.where(key_le_query, st[past:], -jnp.inf)]
                if n_past:
                    parts.insert(0, st[:past])
                m = functools.reduce(jnp.maximum,
                                     [jnp.max(t, axis=0, keepdims=True) for t in parts])
                p_t = jnp.concatenate([jnp.exp2(t - m).astype(BF16) for t in parts], axis=0)
                ones_row = dh if sub == 0 else 0
                own = (row_v < dh) if sub == 0 else (row_v >= dh)
                v_h_t = jnp.where(own, v_t,
                                  jnp.where(row_v == ones_row, 1.0, 0.0).astype(v_t.dtype))
                o_t = jnp.dot(v_h_t, p_t, preferred_element_type=F32)
                outs.append(o_t / o_t[ones_row:ones_row + 1, :])
            o_pair_t = jnp.where(row_o < dh, outs[0], outs[1])
            o_ref[0, past:past + blk, cols] = o_pair_t.T.astype(o_ref.dtype)

    def attend_both(i):
        attend(i)
        attend(n_blk - 1 - i)

    for i in range(n_blk // 2):
        pl.when(step == i)(functools.partial(attend_both, i))


FOX_PAIRS_PER_STEP = 4


def fox_attention(q, k, v, cum_t, batch, seq, blk=FOX_BLOCK):
    width = q.shape[-1]
    pairs = C_HEADS // 2 // FOX_PAIRS_PER_STEP
    pair_w = 2 * C_HEAD_DIM * FOX_PAIRS_PER_STEP
    gates = 2 * FOX_PAIRS_PER_STEP
    cum = jnp.pad(cum_t.reshape(batch, C_HEADS // gates, gates, seq),
                  ((0, 0), (0, 0), (0, SUBLANES - gates), (0, 0)))
    q3, k3, v3 = (t.reshape(batch, seq, width) for t in (q, k, v))
    seq_spec = pl.BlockSpec((1, seq, pair_w), lambda b, p, i: (b, 0, p))
    assert (seq // blk) % 2 == 0
    out = pl.pallas_call(
        functools.partial(_fox_attn_kernel, blk),
        grid=(batch, pairs, seq // blk // 2),
        in_specs=[seq_spec, seq_spec, seq_spec,
                  pl.BlockSpec((1, 1, SUBLANES, seq), lambda b, p, i: (b, p, 0, 0))],
        out_specs=seq_spec,
        out_shape=jax.ShapeDtypeStruct((batch, seq, width), BF16),
        compiler_params=_params("parallel", "parallel", "arbitrary"),
        name="fox_attn",
    )(q3, k3, v3, cum)
    return out.reshape(batch * seq, width)


def _even_mixer(h, gain, w_in_stack, e, lower_bound, out_gain, rel_table, batch, seq):
    a_w, b_w = A_HEADS * A_HEAD_DIM, B_HEADS * B_HEAD_DIM
    splits = [(a_w, BF16, 1.0), (a_w, F32, 1.0), (a_w, BF16, 1.0), (a_w, BF16, 1.0),
              (b_w, BF16, B_HEAD_DIM ** -0.5 * LOG2_E), (b_w, BF16, 1.0), (b_w, BF16, 1.0)]
    q_a, f_a, i_a, g_a, q_b, k_b, v_b = norm_proj(h, gain, w_in_stack, e, splits)
    o_a = hgrn2_mixer(q_a, f_a, i_a, g_a, lower_bound, out_gain, batch, seq)
    o_b = band_attention(q_b, k_b, v_b, rel_bias(rel_table), batch, seq)
    return [o_a, o_b]


def _pad_gate_columns(w_in_c):
    return jnp.pad(w_in_c, ((0, 0), (0, 0), (0, LANES - C_HEADS)))


def _odd_mixer(h, gain, w_in_stack, o, f_bias, batch, seq):
    c_w = C_HEADS * C_HEAD_DIM
    splits = [(c_w, BF16, C_HEAD_DIM ** -0.5 * LOG2_E), (c_w, BF16, 1.0), (c_w, BF16, 1.0),
              (LANES, F32, 1.0)]
    q, k, v, f = norm_proj(h, gain, w_in_stack, o, splits)
    cum_t = fox_cum(f, f_bias, batch, seq)
    return [fox_attention(q, k, v, cum_t, batch, seq)]


def kernel(x, mem, norm_mix, norm_xattn, norm_mem, norm_mlp, norm_final, w_in_ab, a_lb_logits,
           a_out_gain, b_rel_bias, w_out_ab, w_in_c, c_fgate_bias, w_out_c, w_xq, w_xkv, w_xo,
           w_up, w_down):
    batch, seq, d = x.shape
    mem_tokens = mem.shape[1]
    depth = norm_mix.shape[0]
    lb_w = jax.nn.softmax(a_lb_logits.astype(F32), axis=0)
    lower_bounds = jnp.cumsum(lb_w, axis=0) - lb_w[0]
    h = x.reshape(batch * seq, d)
    mem2 = mem.reshape(batch * mem_tokens, d)
    w_in_ab, w_in_c = w_in_ab.astype(BF16), _pad_gate_columns(w_in_c).astype(BF16)
    w_out_ab, w_out_c = w_out_ab.astype(BF16), w_out_c.astype(BF16)
    w_xq, w_xkv, w_xo = w_xq.astype(BF16), w_xkv.astype(BF16), w_xo.astype(BF16)
    w_up, w_down = w_up.astype(BF16), w_down.astype(BF16)
    for layer in range(depth):
        mixer = layer // 2
        if layer % 2 == 0:
            acts = _even_mixer(h, norm_mix[layer], w_in_ab, mixer, lower_bounds[mixer],
                               a_out_gain[mixer], b_rel_bias[mixer], batch, seq)
            w_out = w_out_ab
        else:
            acts = _odd_mixer(h, norm_mix[layer], w_in_c, mixer, c_fgate_bias[mixer], batch, seq)
            w_out = w_out_c
        k_mem, v_mem = norm_proj(mem2, norm_mem[layer], w_xkv, layer,
                                 [(d, BF16, 1.0), (d, BF16, 1.0)])
        h = layer_tail(h, acts, w_out, mixer, norm_xattn[layer], w_xq,
                       k_mem.reshape(batch, mem_tokens, d), v_mem.reshape(batch, mem_tokens, d),
                       w_xo, norm_mlp[layer], w_up, w_down, layer, seq,
                       g_final=norm_final if layer == depth - 1 else None)
    return h.reshape(batch, seq, d)
```
